```python
import math
import jax, jax.numpy as jnp
from jax import lax
import numpy as np

D_MODEL = 1024
BATCH = 8
SEQ = 2048
DEPTH = 4

D_HEAD = 64
A_WIDTH = D_MODEL // 4
B_HEADS = (D_MODEL // 2) // D_HEAD
B_WIDTH = B_HEADS * D_HEAD
C_WIDTH = D_MODEL // 4
D_MIX = A_WIDTH + B_WIDTH + C_WIDTH
SPLIT_SIZES = (A_WIDTH, A_WIDTH, A_WIDTH,
               B_WIDTH, B_WIDTH, B_WIDTH,
               C_WIDTH, C_WIDTH)
IN_COLS = sum(SPLIT_SIZES)
DILATED_BRANCHES = ((128, 1), (512, 4), (2048, 16))
BLK = 128
NUM_BUCKETS = 32
MAX_DISTANCE = 2048
SHORT_CONV = 3
CONFORMER_CONV = 31
FFN_CONV = 3
D_FF = ((8 * D_MODEL // 3 + 127) // 128) * 128
EPS = 1e-6
NEG = -1e30

kernel_name = 'hybrid_shortconv_dilatedattn_conformer_trunk'


def rmsnorm(x, g):
    xf = x.astype(jnp.float32)
    y = xf * lax.rsqrt(jnp.mean(xf * xf, axis=-1, keepdims=True) + EPS)
    return (y * g.astype(jnp.float32)).astype(x.dtype)


def layernorm(x, g, b):
    xf = x.astype(jnp.float32)
    mu = jnp.mean(xf, axis=-1, keepdims=True)
    var = jnp.mean(jnp.square(xf - mu), axis=-1, keepdims=True)
    y = (xf - mu) * lax.rsqrt(var + EPS)
    return (y * g.astype(jnp.float32) + b.astype(jnp.float32)).astype(x.dtype)


def causal_dwconv(x, w):
    k_width, ch = w.shape
    return lax.conv_general_dilated(
        x, w[:, None, :].astype(x.dtype), window_strides=(1,),
        padding=[(k_width - 1, 0)], dimension_numbers=('NWC', 'WIO', 'NWC'),
        feature_group_count=ch)


def t5_bucket(dist):
    max_exact = NUM_BUCKETS // 2
    d_f = jnp.maximum(dist, 1).astype(jnp.float32)
    large = max_exact + (jnp.log(d_f / max_exact) / math.log(MAX_DISTANCE / max_exact)
                         * (NUM_BUCKETS - max_exact)).astype(jnp.int32)
    large = jnp.minimum(large, NUM_BUCKETS - 1)
    return jnp.where(dist < max_exact, dist, large)


def dilated_branch(q, k, v, rel_bias, window, dilation):
    bsz, seq, heads, hd = q.shape
    n_keys = window // dilation
    sub_len = seq // dilation
    n_blk = -(-sub_len // BLK)
    pad = n_blk * BLK - sub_len

    def to_blocks(t):
        t = t.reshape(bsz, sub_len, dilation, heads, hd).transpose(0, 2, 3, 1, 4)
        t = jnp.pad(t, ((0, 0), (0, 0), (0, 0), (0, pad), (0, 0)))
        return t.reshape(bsz, dilation, heads, n_blk, BLK, hd)

    qb, kb, vb = to_blocks(q), to_blocks(k), to_blocks(v)

    def with_prev(t):
        prev = jnp.concatenate([jnp.zeros_like(t[:, :, :, :1]), t[:, :, :, :-1]], axis=3)
        return jnp.concatenate([prev, t], axis=4)

    kk, vv = with_prev(kb), with_prev(vb)
    s = jnp.einsum('bdhnqc,bdhnkc->bdhnqk', qb, kk).astype(jnp.float32) * (hd ** -0.5)
    rel = jnp.arange(BLK)[:, None] - jnp.arange(2 * BLK)[None, :] + BLK
    k_idx = jnp.arange(n_blk)[:, None] * BLK + jnp.arange(2 * BLK)[None, :] - BLK
    valid = ((rel >= 0) & (rel <= n_keys))[None] & (k_idx >= 0)[:, None, :]
    bias = rel_bias[t5_bucket(jnp.maximum(rel, 0) * dilation)]
    bias = bias.transpose(2, 0, 1).astype(jnp.float32)[:, None]
    s = jnp.where(valid, s + bias, NEG)
    m = jnp.max(s, axis=-1, keepdims=True)
    p = jnp.exp(s - m)
    den = jnp.sum(p, axis=-1, keepdims=True)
    o = jnp.einsum('bdhnqk,bdhnkc->bdhnqc', p.astype(v.dtype), vv).astype(jnp.float32) / den
    lse = (m + jnp.log(den))[..., 0]
    o = o.reshape(bsz, dilation, heads, n_blk * BLK, hd)[:, :, :, :sub_len]
    o = o.transpose(0, 3, 1, 2, 4).reshape(bsz, seq, heads, hd)
    lse = lse.reshape(bsz, dilation, heads, n_blk * BLK)[..., :sub_len]
    lse = lse.transpose(0, 3, 1, 2).reshape(bsz, seq, heads)
    return o, lse


def dilated_mixture(q, k, v, rel_bias):
    outs, lses = [], []
    for window, dilation in DILATED_BRANCHES:
        o, l = dilated_branch(q, k, v, rel_bias, window, dilation)
        outs.append(o)
        lses.append(l)
    wts = jax.nn.softmax(jnp.stack(lses, axis=0), axis=0)
    return jnp.sum(wts[..., None] * jnp.stack(outs, axis=0), axis=0)


def setup_inputs(seed: int = 0) -> dict:
    key = jax.random.key(seed)
    ks = jax.random.split(key, 20)
    f32 = jnp.float32

    def nrm(k, shape, scale):
        return jax.random.normal(k, shape, f32) * scale

    return {
        'x': nrm(ks[0], (BATCH, SEQ, D_MODEL), 1.0),
        'norm_mix_g': 1.0 + nrm(ks[1], (DEPTH, D_MODEL), 0.02),
        'w_in': nrm(ks[2], (DEPTH, D_MODEL, IN_COLS), D_MODEL ** -0.5),
        'conv_a_w': nrm(ks[3], (DEPTH, SHORT_CONV, A_WIDTH), SHORT_CONV ** -0.5),
        'conv_c_w': nrm(ks[4], (DEPTH, CONFORMER_CONV, C_WIDTH), CONFORMER_CONV ** -0.5),
        'conv_c_b': nrm(ks[5], (DEPTH, C_WIDTH), 0.02),
        'ln_c_g': 1.0 + nrm(ks[6], (DEPTH, C_WIDTH), 0.02),
        'ln_c_b': nrm(ks[7], (DEPTH, C_WIDTH), 0.02),
        'out_norm_g': 1.0 + nrm(ks[8], (DEPTH, D_MIX), 0.02),
        'w_out': nrm(ks[9], (DEPTH, D_MIX, D_MODEL), D_MIX ** -0.5),
        'norm_ffn_g': 1.0 + nrm(ks[10], (DEPTH, D_MODEL), 0.02),
        'w_up': nrm(ks[11], (DEPTH, D_MODEL, 2 * D_FF), D_MODEL ** -0.5),
        'conv_f_w': nrm(ks[12], (DEPTH, FFN_CONV, 2 * D_FF), FFN_CONV ** -0.5),
        'w_down': nrm(ks[13], (DEPTH, D_FF, D_MODEL), D_FF ** -0.5),
        'rel_bias': nrm(ks[14], (NUM_BUCKETS, B_HEADS), 0.5),
        'final_g': 1.0 + nrm(ks[15], (D_MODEL,), 0.02),
    }


def reference(x, norm_mix_g, w_in, conv_a_w, conv_c_w, conv_c_b, ln_c_g, ln_c_b,
              out_norm_g, w_out, norm_ffn_g, w_up, conv_f_w, w_down, rel_bias, final_g):
    bsz, seq, _ = x.shape
    split_idx = list(np.cumsum(SPLIT_SIZES)[:-1])
    g_idx = [A_WIDTH, A_WIDTH + B_WIDTH]
    for l in range(DEPTH):
        h = rmsnorm(x, norm_mix_g[l])
        z = h @ w_in[l]
        a_h, a_b, a_c, q, k, v, c_val, c_gate = jnp.split(z, split_idx, axis=-1)
        y_a = a_b * causal_dwconv(a_c * a_h, conv_a_w[l])
        hs = (bsz, seq, B_HEADS, D_HEAD)
        y_b = dilated_mixture(q.reshape(hs), k.reshape(hs), v.reshape(hs), rel_bias)
        y_b = y_b.reshape(bsz, seq, B_WIDTH).astype(x.dtype)
        u = c_val * jax.nn.sigmoid(c_gate)
        u = causal_dwconv(u, conv_c_w[l]) + conv_c_b[l].astype(u.dtype)
        y_c = jax.nn.silu(layernorm(u, ln_c_g[l], ln_c_b[l]))
        g_a, g_b, g_c = jnp.split(out_norm_g[l], g_idx)
        y = jnp.concatenate([rmsnorm(y_a, g_a), rmsnorm(y_b, g_b), rmsnorm(y_c, g_c)], axis=-1)
        x = x + y @ w_out[l]
        h = rmsnorm(x, norm_ffn_g[l])
        up = causal_dwconv(h @ w_up[l], conv_f_w[l])
        gate, val = jnp.split(up, 2, axis=-1)
        x = x + (jax.nn.silu(gate) * val) @ w_down[l]
    return rmsnorm(x, final_g)
```

```python
import functools
import math

import numpy as np
import jax
import jax.numpy as jnp
from jax import lax
from jax.experimental import pallas as pl
from jax.experimental.pallas import tpu as pltpu

F32 = jnp.float32
BF16 = jnp.bfloat16

D_MODEL = 1024
D_HEAD = 64
A_WIDTH = 256
B_HEADS = 8
B_WIDTH = 512
C_WIDTH = 256
IN_COLS = 3 * A_WIDTH + 3 * B_WIDTH + 2 * C_WIDTH
QKV_LO = 3 * A_WIDTH
C_LO = QKV_LO + 3 * B_WIDTH
DILATIONS = (1, 4, 16)
N_KEYS = 128
BLK = 128
NUM_BUCKETS = 32
MAX_DISTANCE = 2048
SHORT_CONV = 3
CONFORMER_CONV = 31
FFN_CONV = 3
D_FF = 2816
EPS = 1e-6
NEG = -1e30

LANES = 128
SUBLANES = 8
VMEM_LIMIT = 56 * 1024 * 1024

T_IN = 512
T_FFN = 256
CONV_ROWS = 64
FF_CHUNK = 256
A_CARRY = SUBLANES
C_CARRY = 32


def _rms(x, g):
    return x * lax.rsqrt(jnp.mean(x * x, axis=-1, keepdims=True) + EPS) * g


def _dot(a, b):
    return jnp.dot(a, b, preferred_element_type=F32)


def _bucket_table():
    rel = np.arange(BLK)[:, None] - np.arange(2 * BLK)[None, :] + BLK
    valid = (rel >= 0) & (rel <= N_KEYS)
    max_exact = NUM_BUCKETS // 2
    out = []
    for d in DILATIONS:
        dist = np.maximum(rel, 0) * d
        d_f = np.maximum(dist, 1).astype(np.float32)
        large = max_exact + (np.log(d_f / np.float32(max_exact)) / np.float32(math.log(MAX_DISTANCE / max_exact))
                             * np.float32(NUM_BUCKETS - max_exact)).astype(np.int32)
        large = np.minimum(large, NUM_BUCKETS - 1)
        bucket = np.where(dist < max_exact, dist, large)
        out.append(np.where(valid, bucket, -1).astype(np.int32))
    return np.stack(out)


def _bias_kernel(rb_ref, bucket_ref, o_ref):
    h = pl.program_id(1)
    bucket = bucket_ref[...]
    acc = jnp.full(bucket.shape, NEG, F32)
    for b in range(NUM_BUCKETS):
        acc = jnp.where(bucket == b, rb_ref[b, h], acc)
    o_ref[...] = acc


def _bias_table(rel_bias):
    buckets = jnp.asarray(_bucket_table())
    return pl.pallas_call(
        _bias_kernel,
        grid=(len(DILATIONS), B_HEADS),
        in_specs=[pl.BlockSpec(memory_space=pltpu.SMEM),
                  pl.BlockSpec((None, BLK, 2 * BLK), lambda br, h: (br, 0, 0))],
        out_specs=pl.BlockSpec((None, None, BLK, 2 * BLK), lambda br, h: (br, h, 0, 0)),
        out_shape=jax.ShapeDtypeStruct((len(DILATIONS), B_HEADS, BLK, 2 * BLK), F32),
        name="bias_table",
    )(rel_bias, buckets)


def _inproj_kernel(x_ref, g_ref, w_ref, caw_ref, ccw_ref, ccb_ref, lng_ref, lnb_ref, ga_ref, gc_ref,
                   ya_ref, qkv_ref, yc_ref, tbuf, ubuf, *, tiles_per_seq):
    t_rows = x_ref.shape[0]

    @pl.when(pl.program_id(0) % tiles_per_seq == 0)
    def _():
        tbuf[0:A_CARRY, :] = jnp.zeros((A_CARRY, A_WIDTH), F32)
        ubuf[0:C_CARRY, :] = jnp.zeros((C_CARRY, C_WIDTH), F32)

    h = _rms(x_ref[...], g_ref[...]).astype(BF16)

    za = _dot(h, w_ref[:, 0:QKV_LO])
    a_h, a_b, a_c = za[:, 0:A_WIDTH], za[:, A_WIDTH:2 * A_WIDTH], za[:, 2 * A_WIDTH:3 * A_WIDTH]
    t = a_c * a_h
    tbuf[A_CARRY:A_CARRY + t_rows, :] = t
    caw = caw_ref[...]
    conv = (caw[0:1] * tbuf[A_CARRY - 2:A_CARRY - 2 + t_rows, :]
            + caw[1:2] * tbuf[A_CARRY - 1:A_CARRY - 1 + t_rows, :]
            + caw[2:3] * t)
    tbuf[0:A_CARRY, :] = tbuf[t_rows:t_rows + A_CARRY, :]
    ya_ref[...] = _rms(a_b * conv, ga_ref[...]).astype(BF16)

    zqkv = _dot(h, w_ref[:, QKV_LO:C_LO])
    qkv_ref[:, 0:B_WIDTH] = zqkv[:, 0:B_WIDTH] * (D_HEAD ** -0.5)
    qkv_ref[:, B_WIDTH:3 * B_WIDTH] = zqkv[:, B_WIDTH:3 * B_WIDTH]

    zc = _dot(h, w_ref[:, C_LO:IN_COLS])
    ubuf[C_CARRY:C_CARRY + t_rows, :] = zc[:, 0:C_WIDTH] * jax.nn.sigmoid(zc[:, C_WIDTH:2 * C_WIDTH])
    ccw = ccw_ref[...]
    first = C_CARRY - (CONFORMER_CONV - 1)
    for rc in range(t_rows // CONV_ROWS):
        acc = jnp.broadcast_to(ccb_ref[...], (CONV_ROWS, C_WIDTH))
        for k in range(CONFORMER_CONV):
            acc = acc + ccw[k:k + 1] * ubuf[pl.ds(rc * CONV_ROWS + first + k, CONV_ROWS), :]
        mu = jnp.mean(acc, axis=-1, keepdims=True)
        cen = acc - mu
        var = jnp.mean(cen * cen, axis=-1, keepdims=True)
        ln = cen * lax.rsqrt(var + EPS) * lng_ref[...] + lnb_ref[...]
        yc = ln * jax.nn.sigmoid(ln)
        yc_ref[rc * CONV_ROWS:(rc + 1) * CONV_ROWS, :] = _rms(yc, gc_ref[...]).astype(BF16)
    ubuf[0:C_CARRY, :] = ubuf[t_rows:t_rows + C_CARRY, :]


def _inproj(x, layer, seq, g, w_in, caw, ccw, ccb, lng, lnb, ga, gc):
    n = x.shape[0]
    row = lambda width: pl.BlockSpec((None, 1, width), lambda i: (layer, 0, 0))
    return pl.pallas_call(
        functools.partial(_inproj_kernel, tiles_per_seq=seq // T_IN),
        grid=(n // T_IN,),
        in_specs=[pl.BlockSpec((T_IN, D_MODEL), lambda i: (i, 0)),
                  row(D_MODEL),
                  pl.BlockSpec((None, D_MODEL, IN_COLS), lambda i: (layer, 0, 0), pipeline_mode=pl.Buffered(1)),
                  pl.BlockSpec((None, SHORT_CONV, A_WIDTH), lambda i: (layer, 0, 0)),
                  pl.BlockSpec((None, CONFORMER_CONV, C_WIDTH), lambda i: (layer, 0, 0)),
                  row(C_WIDTH), row(C_WIDTH), row(C_WIDTH), row(A_WIDTH), row(C_WIDTH)],
        out_specs=[pl.BlockSpec((T_IN, A_WIDTH), lambda i: (i, 0)),
                   pl.BlockSpec((T_IN, 3 * B_WIDTH), lambda i: (i, 0)),
                   pl.BlockSpec((T_IN, C_WIDTH), lambda i: (i, 0))],
        out_shape=[jax.ShapeDtypeStruct((n, A_WIDTH), BF16),
                   jax.ShapeDtypeStruct((n, 3 * B_WIDTH), F32),
                   jax.ShapeDtypeStruct((n, C_WIDTH), BF16)],
        scratch_shapes=[pltpu.VMEM((A_CARRY + T_IN, A_WIDTH), F32),
                        pltpu.VMEM((C_CARRY + T_IN, C_WIDTH), F32)],
        compiler_params=pltpu.CompilerParams(dimension_semantics=("arbitrary",), vmem_limit_bytes=VMEM_LIMIT),
        name="inproj_mix",
    )(x, g, w_in, caw, ccw, ccb, lng, lnb, ga, gc)


def _attn_kernel(q_ref, k_ref, v_ref, bias_ref, o_ref, q0s, q1s, ks, vs, *stats):
    seq = q_ref.shape[0]
    n_blocks = seq // BLK
    low = lax.broadcasted_iota(jnp.int32, (1, LANES), 1) < D_HEAD
    nt = (((1,), (1,)), ((), ()))

    for br, d in enumerate(DILATIONS):
        acc_s, m_s, l_s = stats[3 * br:3 * br + 3]
        sub = seq // d
        for r in range(d):
            src = pl.ds(r, sub, stride=d) if d > 1 else pl.ds(0, sub)
            dst = pl.ds(r * sub, sub)
            qv = q_ref[src, :]
            q0s[dst, :] = jnp.where(low, qv, 0.0).astype(BF16)
            q1s[dst, :] = jnp.where(low, 0.0, qv).astype(BF16)
            ks[dst, :] = k_ref[src, :].astype(BF16)
            vs[dst, :] = v_ref[src, :].astype(BF16)
        per_class = sub // BLK
        for b in range(n_blocks):
            r, j = divmod(b, per_class)
            rows = pl.ds(b * BLK, BLK)
            keys = rows if j == 0 else pl.ds((b - 1) * BLK, 2 * BLK)
            kw = ks[keys, :]
            vw = vs[keys, :]
            res = []
            for hh, qs in enumerate((q0s, q1s)):
                bias = bias_ref[br, hh, :, BLK:2 * BLK] if j == 0 else bias_ref[br, hh]
                s = lax.dot_general(qs[rows, :], kw, nt, preferred_element_type=F32) + bias
                m = jnp.max(s, axis=-1, keepdims=True)
                p = jnp.exp(s - m)
                l = jnp.sum(p, axis=-1, keepdims=True)
                res.append((_dot(p.astype(BF16), vw), m, l))
            dst = pl.ds(d * BLK * j + r, BLK, stride=d) if d > 1 else rows
            acc_s[dst, :] = jnp.where(low, res[0][0], res[1][0])
            m_s[dst, :] = jnp.where(low, res[0][1], res[1][1])
            l_s[dst, :] = jnp.where(low, res[0][2], res[1][2])

    for t in range(n_blocks):
        rows = pl.ds(t * BLK, BLK)
        ms = [stats[3 * br + 1][rows, :] for br in range(len(DILATIONS))]
        m_all = jnp.maximum(jnp.maximum(ms[0], ms[1]), ms[2])
        num = jnp.zeros((BLK, LANES), F32)
        den = jnp.zeros((BLK, LANES), F32)
        for br in range(len(DILATIONS)):
            e = jnp.exp(ms[br] - m_all)
            num = num + e * stats[3 * br][rows, :]
            den = den + e * stats[3 * br + 2][rows, :]
        o_ref[rows, :] = num / den


def _attention(qkv, bias_tab, batch, seq):
    qkv3 = qkv.reshape(batch, seq, 3 * B_WIDTH)
    pairs = B_WIDTH // LANES
    blk = lambda off: pl.BlockSpec((None, seq, LANES), lambda g, b: (b, 0, off * pairs + g))
    return pl.pallas_call(
        _attn_kernel,
        grid=(pairs, batch),
        in_specs=[blk(0), blk(1), blk(2),
                  pl.BlockSpec((len(DILATIONS), 2, BLK, 2 * BLK), lambda g, b: (0, g, 0, 0))],
        out_specs=pl.BlockSpec((None, seq, LANES), lambda g, b: (b, 0, g)),
        out_shape=jax.ShapeDtypeStruct((batch, seq, B_WIDTH), F32),
        scratch_shapes=[pltpu.VMEM((seq, LANES), BF16)] * 4 + [pltpu.VMEM((seq, LANES), F32)] * (3 * len(DILATIONS)),
        compiler_params=pltpu.CompilerParams(dimension_semantics=("arbitrary", "arbitrary"),
                                             vmem_limit_bytes=VMEM_LIMIT),
        name="dilated_attn",
    )(qkv3, qkv3, qkv3, bias_tab)


def _ffn_kernel(x_ref, ya_ref, yb_ref, yc_ref, gb_ref, wout_ref, gf_ref, wup_ref, cfw_ref, wdn_ref, fg_ref,
                o_ref, ubuf, act, *, tiles_per_seq, final):
    t_rows = x_ref.shape[0]

    @pl.when(pl.program_id(0) % tiles_per_seq == 0)
    def _():
        ubuf[0:A_CARRY, :] = jnp.zeros((A_CARRY, 2 * D_FF), F32)

    y = jnp.concatenate([ya_ref[...], _rms(yb_ref[...], gb_ref[...]).astype(BF16), yc_ref[...]], axis=-1)
    x1 = x_ref[...] + _dot(y, wout_ref[...])
    h = _rms(x1, gf_ref[...]).astype(BF16)

    def conv3(c0):
        cols = pl.ds(c0, FF_CHUNK)
        up = _dot(h, wup_ref[:, cols])
        ubuf[A_CARRY:A_CARRY + t_rows, cols] = up
        w = cfw_ref[:, cols]
        out = (w[0:1] * ubuf[A_CARRY - 2:A_CARRY - 2 + t_rows, cols]
               + w[1:2] * ubuf[A_CARRY - 1:A_CARRY - 1 + t_rows, cols]
               + w[2:3] * up)
        ubuf[0:A_CARRY, cols] = ubuf[t_rows:t_rows + A_CARRY, cols]
        return out

    for c in range(D_FF // FF_CHUNK):
        gate = conv3(c * FF_CHUNK)
        val = conv3(D_FF + c * FF_CHUNK)
        act[:, c * FF_CHUNK:(c + 1) * FF_CHUNK] = (gate * jax.nn.sigmoid(gate) * val).astype(BF16)

    x2 = x1 + _dot(act[...], wdn_ref[...])
    if final:
        x2 = _rms(x2, fg_ref[...])
    o_ref[...] = x2


def _ffn(x, ya, yb, yc, layer, seq, final, gb, w_out, gf, w_up, cfw, w_dn, fg):
    n = x.shape[0]
    tile = lambda width: pl.BlockSpec((T_FFN, width), lambda i: (i, 0))
    row = lambda width: pl.BlockSpec((None, 1, width), lambda i: (layer, 0, 0))
    weight = lambda r, c: pl.BlockSpec((None, r, c), lambda i: (layer, 0, 0), pipeline_mode=pl.Buffered(1))
    return pl.pallas_call(
        functools.partial(_ffn_kernel, tiles_per_seq=seq // T_FFN, final=final),
        grid=(n // T_FFN,),
        in_specs=[tile(D_MODEL), tile(A_WIDTH), tile(B_WIDTH), tile(C_WIDTH),
                  row(B_WIDTH), weight(D_MODEL, D_MODEL), row(D_MODEL), weight(D_MODEL, 2 * D_FF),
                  pl.BlockSpec((None, FFN_CONV, 2 * D_FF), lambda i: (layer, 0, 0)),
                  weight(D_FF, D_MODEL),
                  pl.BlockSpec((1, D_MODEL), lambda i: (0, 0))],
        out_specs=tile(D_MODEL),
        out_shape=jax.ShapeDtypeStruct((n, D_MODEL), F32),
        scratch_shapes=[pltpu.VMEM((A_CARRY + T_FFN, 2 * D_FF), F32),
                        pltpu.VMEM((T_FFN, D_FF), BF16)],
        compiler_params=pltpu.CompilerParams(dimension_semantics=("arbitrary",), vmem_limit_bytes=VMEM_LIMIT),
        name="outproj_ffn",
    )(x, ya, yb, yc, gb, w_out, gf, w_up, cfw, w_dn, fg)


def kernel(x, norm_mix_g, w_in, conv_a_w, conv_c_w, conv_c_b, ln_c_g, ln_c_b, out_norm_g, w_out, norm_ffn_g, w_up, conv_f_w, w_down, rel_bias, final_g):
    batch, seq, d_model = x.shape
    depth = w_in.shape[0]
    assert d_model == D_MODEL and seq % T_IN == 0 and seq % T_FFN == 0 and seq % (BLK * max(DILATIONS)) == 0
    assert w_in.shape[2] == IN_COLS and w_up.shape[2] == 2 * D_FF and D_FF % FF_CHUNK == 0

    rows = lambda a: a.reshape(depth, 1, a.shape[-1])
    w_in_b, w_out_b, w_up_b, w_dn_b = (w.astype(BF16) for w in (w_in, w_out, w_up, w_down))
    g_mix, g_ffn, ccb, lng, lnb = (rows(a) for a in (norm_mix_g, norm_ffn_g, conv_c_b, ln_c_g, ln_c_b))
    g_a = rows(out_norm_g[:, 0:A_WIDTH])
    g_b = rows(out_norm_g[:, A_WIDTH:A_WIDTH + B_WIDTH])
    g_c = rows(out_norm_g[:, A_WIDTH + B_WIDTH:])
    fg = final_g.reshape(1, D_MODEL)

    bias_tab = _bias_table(rel_bias)
    xf = x.reshape(batch * seq, D_MODEL)
    for layer in range(depth):
        ya, qkv, yc = _inproj(xf, layer, seq, g_mix, w_in_b, conv_a_w, conv_c_w, ccb, lng, lnb, g_a, g_c)
        yb = _attention(qkv, bias_tab, batch, seq).reshape(batch * seq, B_WIDTH)
        xf = _ffn(xf, ya, yb, yc, layer, seq, layer == depth - 1, g_b, w_out_b, g_ffn, w_up_b, conv_f_w, w_dn_b, fg)
    return xf.reshape(batch, seq, D_MODEL)
```

```python
import functools
import math

import numpy as np
import jax
import jax.numpy as jnp
from jax import lax
from jax.experimental import pallas as pl
from jax.experimental.pallas import tpu as pltpu

F32 = jnp.float32
BF16 = jnp.bfloat16

D_MODEL = 1024
D_HEAD = 64
A_WIDTH = 256
B_HEADS = 8
B_WIDTH = 512
C_WIDTH = 256
IN_COLS = 3 * A_WIDTH + 3 * B_WIDTH + 2 * C_WIDTH
QKV_LO = 3 * A_WIDTH
C_LO = QKV_LO + 3 * B_WIDTH
DILATIONS = (1, 4, 16)
N_KEYS = 128
BLK = 128
NUM_BUCKETS = 32
MAX_DISTANCE = 2048
SHORT_CONV = 3
CONFORMER_CONV = 31
FFN_CONV = 3
D_FF = 2816
EPS = 1e-6
NEG = -1e30

LANES = 128
SUBLANES = 8
VMEM_LIMIT = 56 * 1024 * 1024

T_IN = 512
T_FFN = 256
CONV_ROWS = 64
FF_CHUNK = 256
A_CARRY = SUBLANES
C_CARRY = 32


def _rms(x, g):
    return x * lax.rsqrt(jnp.mean(x * x, axis=-1, keepdims=True) + EPS) * g


def _dot(a, b):
    return jnp.dot(a, b, preferred_element_type=F32)


def _bucket_table():
    rel = np.arange(BLK)[:, None] - np.arange(2 * BLK)[None, :] + BLK
    valid = (rel >= 0) & (rel <= N_KEYS)
    max_exact = NUM_BUCKETS // 2
    out = []
    for d in DILATIONS:
        dist = np.maximum(rel, 0) * d
        d_f = np.maximum(dist, 1).astype(np.float32)
        large = max_exact + (np.log(d_f / np.float32(max_exact)) / np.float32(math.log(MAX_DISTANCE / max_exact))
                             * np.float32(NUM_BUCKETS - max_exact)).astype(np.int32)
        large = np.minimum(large, NUM_BUCKETS - 1)
        bucket = np.where(dist < max_exact, dist, large)
        out.append(np.where(valid, bucket, -1).astype(np.int32))
    return np.stack(out)


def _bias_kernel(rb_ref, bucket_ref, o_ref):
    h = pl.program_id(1)
    bucket = bucket_ref[...]
    acc = jnp.full(bucket.shape, NEG, F32)
    for b in range(NUM_BUCKETS):
        acc = jnp.where(bucket == b, rb_ref[b, h], acc)
    o_ref[...] = acc


def _bias_table(rel_bias):
    buckets = jnp.asarray(_bucket_table())
    return pl.pallas_call(
        _bias_kernel,
        grid=(len(DILATIONS), B_HEADS),
        in_specs=[pl.BlockSpec(memory_space=pltpu.SMEM),
                  pl.BlockSpec((None, BLK, 2 * BLK), lambda br, h: (br, 0, 0))],
        out_specs=pl.BlockSpec((None, None, BLK, 2 * BLK), lambda br, h: (br, h, 0, 0)),
        out_shape=jax.ShapeDtypeStruct((len(DILATIONS), B_HEADS, BLK, 2 * BLK), F32),
        name="bias_table",
    )(rel_bias, buckets)


def _inproj_kernel(x_ref, g_ref, w_ref, caw_ref, ccw_ref, ccb_ref, lng_ref, lnb_ref, ga_ref, gc_ref,
                   ya_ref, qkv_ref, yc_ref, tbuf, ubuf, *, tiles_per_seq):
    t_rows = x_ref.shape[0]

    @pl.when(pl.program_id(0) % tiles_per_seq == 0)
    def _():
        tbuf[0:A_CARRY, :] = jnp.zeros((A_CARRY, A_WIDTH), F32)
        ubuf[0:C_CARRY, :] = jnp.zeros((C_CARRY, C_WIDTH), F32)

    h = _rms(x_ref[...], g_ref[...]).astype(BF16)

    za = _dot(h, w_ref[:, 0:QKV_LO])
    a_h, a_b, a_c = za[:, 0:A_WIDTH], za[:, A_WIDTH:2 * A_WIDTH], za[:, 2 * A_WIDTH:3 * A_WIDTH]
    t = a_c * a_h
    tbuf[A_CARRY:A_CARRY + t_rows, :] = t
    caw = caw_ref[...]
    conv = (caw[0:1] * tbuf[A_CARRY - 2:A_CARRY - 2 + t_rows, :]
            + caw[1:2] * tbuf[A_CARRY - 1:A_CARRY - 1 + t_rows, :]
            + caw[2:3] * t)
    tbuf[0:A_CARRY, :] = tbuf[t_rows:t_rows + A_CARRY, :]
    ya_ref[...] = _rms(a_b * conv, ga_ref[...]).astype(BF16)

    zqkv = _dot(h, w_ref[:, QKV_LO:C_LO])
    qkv_ref[:, 0:B_WIDTH] = zqkv[:, 0:B_WIDTH] * (D_HEAD ** -0.5)
    qkv_ref[:, B_WIDTH:3 * B_WIDTH] = zqkv[:, B_WIDTH:3 * B_WIDTH]

    zc = _dot(h, w_ref[:, C_LO:IN_COLS])
    ubuf[C_CARRY:C_CARRY + t_rows, :] = zc[:, 0:C_WIDTH] * jax.nn.sigmoid(zc[:, C_WIDTH:2 * C_WIDTH])
    ccw = ccw_ref[...]
    first = C_CARRY - (CONFORMER_CONV - 1)
    for rc in range(t_rows // CONV_ROWS):
        acc = jnp.broadcast_to(ccb_ref[...], (CONV_ROWS, C_WIDTH))
        for k in range(CONFORMER_CONV):
            acc = acc + ccw[k:k + 1] * ubuf[pl.ds(rc * CONV_ROWS + first + k, CONV_ROWS), :]
        mu = jnp.mean(acc, axis=-1, keepdims=True)
        cen = acc - mu
        var = jnp.mean(cen * cen, axis=-1, keepdims=True)
        ln = cen * lax.rsqrt(var + EPS) * lng_ref[...] + lnb_ref[...]
        yc = ln * jax.nn.sigmoid(ln)
        yc_ref[rc * CONV_ROWS:(rc + 1) * CONV_ROWS, :] = _rms(yc, gc_ref[...]).astype(BF16)
    ubuf[0:C_CARRY, :] = ubuf[t_rows:t_rows + C_CARRY, :]


def _inproj(x, layer, seq, g, w_in, caw, ccw, ccb, lng, lnb, ga, gc):
    n = x.shape[0]
    row = lambda width: pl.BlockSpec((None, 1, width), lambda i: (layer, 0, 0))
    return pl.pallas_call(
        functools.partial(_inproj_kernel, tiles_per_seq=seq // T_IN),
        grid=(n // T_IN,),
        in_specs=[pl.BlockSpec((T_IN, D_MODEL), lambda i: (i, 0)),
                  row(D_MODEL),
                  pl.BlockSpec((None, D_MODEL, IN_COLS), lambda i: (layer, 0, 0), pipeline_mode=pl.Buffered(1)),
                  pl.BlockSpec((None, SHORT_CONV, A_WIDTH), lambda i: (layer, 0, 0)),
                  pl.BlockSpec((None, CONFORMER_CONV, C_WIDTH), lambda i: (layer, 0, 0)),
                  row(C_WIDTH), row(C_WIDTH), row(C_WIDTH), row(A_WIDTH), row(C_WIDTH)],
        out_specs=[pl.BlockSpec((T_IN, A_WIDTH), lambda i: (i, 0)),
                   pl.BlockSpec((T_IN, 3 * B_WIDTH), lambda i: (i, 0)),
                   pl.BlockSpec((T_IN, C_WIDTH), lambda i: (i, 0))],
        out_shape=[jax.ShapeDtypeStruct((n, A_WIDTH), BF16),
                   jax.ShapeDtypeStruct((n, 3 * B_WIDTH), F32),
                   jax.ShapeDtypeStruct((n, C_WIDTH), BF16)],
        scratch_shapes=[pltpu.VMEM((A_CARRY + T_IN, A_WIDTH), F32),
                        pltpu.VMEM((C_CARRY + T_IN, C_WIDTH), F32)],
        compiler_params=pltpu.CompilerParams(dimension_semantics=("arbitrary",), vmem_limit_bytes=VMEM_LIMIT),
        name="inproj_mix",
    )(x, g, w_in, caw, ccw, ccb, lng, lnb, ga, gc)


def _attn_kernel(q_ref, k_ref, v_ref, bias_ref, o_ref, g4q, g4k, g4v, q0s, q1s, ks, v0s, v1s, s_buf, p_buf, *stats):
    seq = q_ref.shape[0]
    n_blocks = seq // BLK
    quarter = seq // 4
    low = lax.broadcasted_iota(jnp.int32, (1, LANES), 1) < D_HEAD
    nt = (((1,), (1,)), ((), ()))
    q_masked = (q0s, q1s)
    v_ones = (v0s, v1s)

    def fill(dst, qv, kv, vv):
        q0s[dst, :] = jnp.where(low, qv, 0.0).astype(BF16)
        q1s[dst, :] = jnp.where(low, 0.0, qv).astype(BF16)
        ks[dst, :] = kv.astype(BF16)
        v0s[dst, :] = jnp.where(low, vv, 1.0).astype(BF16)
        v1s[dst, :] = jnp.where(low, 1.0, vv).astype(BF16)

    for br, d in enumerate(DILATIONS):
        if d == 1:
            for c in range(4):
                rows = pl.ds(c * quarter, quarter)
                fill(rows, q_ref[rows, :], k_ref[rows, :], v_ref[rows, :])
        elif d == 4:
            for r in range(4):
                src, dst = pl.ds(r, quarter, stride=4), pl.ds(r * quarter, quarter)
                qv, kv, vv = q_ref[src, :], k_ref[src, :], v_ref[src, :]
                g4q[dst, :], g4k[dst, :], g4v[dst, :] = qv, kv, vv
                fill(dst, qv, kv, vv)
        else:
            for r in range(16):
                src, dst = pl.ds((r % 4) * quarter + r // 4, BLK, stride=4), pl.ds(r * BLK, BLK)
                fill(dst, g4q[src, :], g4k[src, :], g4v[src, :])

        per_class = seq // d // BLK
        blocks = []
        for b in range(n_blocks):
            r, j = divmod(b, per_class)
            rows = pl.ds(b * BLK, BLK)
            keys = rows if j == 0 else pl.ds((b - 1) * BLK, 2 * BLK)
            cols = pl.ds(BLK, BLK) if j == 0 else pl.ds(0, 2 * BLK)
            out = pl.ds((r % 4) * quarter + r // 4, BLK, stride=4) if d == 16 else rows
            blocks.append((b, rows, keys, cols, out))

        for b, rows, keys, cols, out in blocks:
            for hh in range(2):
                s_buf[hh, b, :, cols] = (lax.dot_general(q_masked[hh][rows, :], ks[keys, :], nt,
                                                         preferred_element_type=F32) + bias_ref[br, hh, :, cols])
        for b, rows, keys, cols, out in blocks:
            for hh in range(2):
                s = s_buf[hh, b, :, cols]
                m = jnp.max(s, axis=-1, keepdims=True)
                p_buf[hh, b, :, cols] = jnp.exp(s - m).astype(BF16)
                stats[2 * (2 * br + hh) + 1][out, :] = jnp.broadcast_to(m, (BLK, LANES))
        for b, rows, keys, cols, out in blocks:
            for hh in range(2):
                stats[2 * (2 * br + hh)][out, :] = _dot(p_buf[hh, b, :, cols], v_ones[hh][keys, :])

    for t in range(n_blocks):
        r4, n = divmod(t, 4)
        rows4 = pl.ds(t * BLK, BLK)
        nat = pl.ds(4 * BLK * n + r4, BLK, stride=4)
        y = []
        for hh in range(2):
            where = (nat, rows4, rows4)
            ms = [stats[2 * (2 * br + hh) + 1][where[br], :] for br in range(len(DILATIONS))]
            m_all = jnp.maximum(jnp.maximum(ms[0], ms[1]), ms[2])
            tot = jnp.zeros((BLK, LANES), F32)
            for br in range(len(DILATIONS)):
                tot = tot + jnp.exp(ms[br] - m_all) * stats[2 * (2 * br + hh)][where[br], :]
            y.append(tot / pltpu.roll(tot, D_HEAD, axis=1))
        o_ref[nat, :] = jnp.where(low, y[0], y[1])


def _attention(qkv, bias_tab, batch, seq):
    qkv3 = qkv.reshape(batch, seq, 3 * B_WIDTH)
    pairs = B_WIDTH // LANES
    n_br = len(DILATIONS)
    blk = lambda off: pl.BlockSpec((None, seq, LANES), lambda g, b: (b, 0, off * pairs + g))
    return pl.pallas_call(
        _attn_kernel,
        grid=(pairs, batch),
        in_specs=[blk(0), blk(1), blk(2),
                  pl.BlockSpec((n_br, 2, BLK, 2 * BLK), lambda g, b: (0, g, 0, 0))],
        out_specs=pl.BlockSpec((None, seq, LANES), lambda g, b: (b, 0, g)),
        out_shape=jax.ShapeDtypeStruct((batch, seq, B_WIDTH), F32),
        scratch_shapes=([pltpu.VMEM((seq, LANES), F32)] * 3 + [pltpu.VMEM((seq, LANES), BF16)] * 5
                        + [pltpu.VMEM((2, seq // BLK, BLK, 2 * BLK), F32),
                           pltpu.VMEM((2, seq // BLK, BLK, 2 * BLK), BF16)]
                        + [pltpu.VMEM((seq, LANES), F32)] * (4 * n_br)),
        compiler_params=pltpu.CompilerParams(dimension_semantics=("arbitrary", "arbitrary"),
                                             vmem_limit_bytes=VMEM_LIMIT),
        name="dilated_attn",
    )(qkv3, qkv3, qkv3, bias_tab)


def _ffn_kernel(x_ref, ya_ref, yb_ref, yc_ref, gb_ref, wout_ref, gf_ref, wup_ref, cfw_ref, wdn_ref, fg_ref,
                o_ref, ubuf, act, *, tiles_per_seq, final):
    t_rows = x_ref.shape[0]

    @pl.when(pl.program_id(0) % tiles_per_seq == 0)
    def _():
        ubuf[0:A_CARRY, :] = jnp.zeros((A_CARRY, 2 * D_FF), F32)

    y = jnp.concatenate([ya_ref[...], _rms(yb_ref[...], gb_ref[...]).astype(BF16), yc_ref[...]], axis=-1)
    x1 = x_ref[...] + _dot(y, wout_ref[...])
    h = _rms(x1, gf_ref[...]).astype(BF16)

    def conv3(c0):
        cols = pl.ds(c0, FF_CHUNK)
        up = _dot(h, wup_ref[:, cols])
        ubuf[A_CARRY:A_CARRY + t_rows, cols] = up
        w = cfw_ref[:, cols]
        out = (w[0:1] * ubuf[A_CARRY - 2:A_CARRY - 2 + t_rows, cols]
               + w[1:2] * ubuf[A_CARRY - 1:A_CARRY - 1 + t_rows, cols]
               + w[2:3] * up)
        ubuf[0:A_CARRY, cols] = ubuf[t_rows:t_rows + A_CARRY, cols]
        return out

    for c in range(D_FF // FF_CHUNK):
        gate = conv3(c * FF_CHUNK)
        val = conv3(D_FF + c * FF_CHUNK)
        act[:, c * FF_CHUNK:(c + 1) * FF_CHUNK] = (gate * jax.nn.sigmoid(gate) * val).astype(BF16)

    x2 = x1 + _dot(act[...], wdn_ref[...])
    if final:
        x2 = _rms(x2, fg_ref[...])
    o_ref[...] = x2


def _ffn(x, ya, yb, yc, layer, seq, final, gb, w_out, gf, w_up, cfw, w_dn, fg):
    n = x.shape[0]
    tile = lambda width: pl.BlockSpec((T_FFN, width), lambda i: (i, 0))
    row = lambda width: pl.BlockSpec((None, 1, width), lambda i: (layer, 0, 0))
    weight = lambda r, c: pl.BlockSpec((None, r, c), lambda i: (layer, 0, 0), pipeline_mode=pl.Buffered(1))
    return pl.pallas_call(
        functools.partial(_ffn_kernel, tiles_per_seq=seq // T_FFN, final=final),
        grid=(n // T_FFN,),
        in_specs=[tile(D_MODEL), tile(A_WIDTH), tile(B_WIDTH), tile(C_WIDTH),
                  row(B_WIDTH), weight(D_MODEL, D_MODEL), row(D_MODEL), weight(D_MODEL, 2 * D_FF),
                  pl.BlockSpec((None, FFN_CONV, 2 * D_FF), lambda i: (layer, 0, 0)),
                  weight(D_FF, D_MODEL),
                  pl.BlockSpec((1, D_MODEL), lambda i: (0, 0))],
        out_specs=tile(D_MODEL),
        out_shape=jax.ShapeDtypeStruct((n, D_MODEL), F32),
        scratch_shapes=[pltpu.VMEM((A_CARRY + T_FFN, 2 * D_FF), F32),
                        pltpu.VMEM((T_FFN, D_FF), BF16)],
        compiler_params=pltpu.CompilerParams(dimension_semantics=("arbitrary",), vmem_limit_bytes=VMEM_LIMIT),
        name="outproj_ffn",
    )(x, ya, yb, yc, gb, w_out, gf, w_up, cfw, w_dn, fg)


def kernel(x, norm_mix_g, w_in, conv_a_w, conv_c_w, conv_c_b, ln_c_g, ln_c_b, out_norm_g, w_out, norm_ffn_g, w_up, conv_f_w, w_down, rel_bias, final_g):
    batch, seq, d_model = x.shape
    depth = w_in.shape[0]
    assert d_model == D_MODEL and seq % T_IN == 0 and seq % T_FFN == 0 and seq % (BLK * max(DILATIONS)) == 0
    assert w_in.shape[2] == IN_COLS and w_up.shape[2] == 2 * D_FF and D_FF % FF_CHUNK == 0

    rows = lambda a: a.reshape(depth, 1, a.shape[-1])
    w_in_b, w_out_b, w_up_b, w_dn_b = (w.astype(BF16) for w in (w_in, w_out, w_up, w_down))
    g_mix, g_ffn, ccb, lng, lnb = (rows(a) for a in (norm_mix_g, norm_ffn_g, conv_c_b, ln_c_g, ln_c_b))
    g_a = rows(out_norm_g[:, 0:A_WIDTH])
    g_b = rows(out_norm_g[:, A_WIDTH:A_WIDTH + B_WIDTH])
    g_c = rows(out_norm_g[:, A_WIDTH + B_WIDTH:])
    fg = final_g.reshape(1, D_MODEL)

    bias_tab = _bias_table(rel_bias)
    xf = x.reshape(batch * seq, D_MODEL)
    for layer in range(depth):
        ya, qkv, yc = _inproj(xf, layer, seq, g_mix, w_in_b, conv_a_w, conv_c_w, ccb, lng, lnb, g_a, g_c)
        yb = _attention(qkv, bias_tab, batch, seq).reshape(batch * seq, B_WIDTH)
        xf = _ffn(xf, ya, yb, yc, layer, seq, layer == depth - 1, g_b, w_out_b, g_ffn, w_up_b, conv_f_w, w_dn_b, fg)
    return xf.reshape(batch, seq, D_MODEL)
```

```python
import functools
import math

import numpy as np
import jax
import jax.numpy as jnp
from jax import lax
from jax.experimental import pallas as pl
from jax.experimental.pallas import tpu as pltpu

F32 = jnp.float32
BF16 = jnp.bfloat16

D_MODEL = 1024
D_HEAD = 64
A_WIDTH = 256
B_HEADS = 8
B_WIDTH = 512
C_WIDTH = 256
IN_COLS = 3 * A_WIDTH + 3 * B_WIDTH + 2 * C_WIDTH
QKV_LO = 3 * A_WIDTH
C_LO = QKV_LO + 3 * B_WIDTH
DILATIONS = (1, 4, 16)
N_KEYS = 128
BLK = 128
NUM_BUCKETS = 32
MAX_DISTANCE = 2048
SHORT_CONV = 3
CONFORMER_CONV = 31
FFN_CONV = 3
D_FF = 2816
EPS = 1e-6
NEG = -1e30

LANES = 128
SUBLANES = 8
VMEM_LIMIT = 56 * 1024 * 1024

T_IN = 512
T_FFN = 512
CONV_ROWS = 64
FF_CHUNK = 256
A_CARRY = SUBLANES
C_CARRY = 32


def _rms(x, g):
    return x * lax.rsqrt(jnp.mean(x * x, axis=-1, keepdims=True) + EPS) * g


def _dot(a, b):
    return jnp.dot(a, b, preferred_element_type=F32)


def _bucket_table():
    rel = np.arange(BLK)[:, None] - np.arange(2 * BLK)[None, :] + BLK
    valid = (rel >= 0) & (rel <= N_KEYS)
    max_exact = NUM_BUCKETS // 2
    out = []
    for d in DILATIONS:
        dist = np.maximum(rel, 0) * d
        d_f = np.maximum(dist, 1).astype(np.float32)
        large = max_exact + (np.log(d_f / np.float32(max_exact)) / np.float32(math.log(MAX_DISTANCE / max_exact))
                             * np.float32(NUM_BUCKETS - max_exact)).astype(np.int32)
        large = np.minimum(large, NUM_BUCKETS - 1)
        bucket = np.where(dist < max_exact, dist, large)
        out.append(np.where(valid, bucket, -1).astype(np.int32))
    return np.stack(out)


def _bias_kernel(rb_ref, bucket_ref, o_ref):
    h = pl.program_id(1)
    bucket = bucket_ref[...]
    acc = jnp.full(bucket.shape, NEG, F32)
    for b in range(NUM_BUCKETS):
        acc = jnp.where(bucket == b, rb_ref[b, h], acc)
    o_ref[...] = acc


def _bias_table(rel_bias):
    buckets = jnp.asarray(_bucket_table())
    return pl.pallas_call(
        _bias_kernel,
        grid=(len(DILATIONS), B_HEADS),
        in_specs=[pl.BlockSpec(memory_space=pltpu.SMEM),
                  pl.BlockSpec((None, BLK, 2 * BLK), lambda br, h: (br, 0, 0))],
        out_specs=pl.BlockSpec((None, None, BLK, 2 * BLK), lambda br, h: (br, h, 0, 0)),
        out_shape=jax.ShapeDtypeStruct((len(DILATIONS), B_HEADS, BLK, 2 * BLK), F32),
        name="bias_table",
    )(rel_bias, buckets)


def _inproj_kernel(x_ref, g_ref, w_ref, caw_ref, ccw_ref, ccb_ref, lng_ref, lnb_ref, ga_ref, gc_ref,
                   ya_ref, qkv_ref, yc_ref, tbuf, ubuf, ushift, *, tiles_per_seq):
    t_rows = x_ref.shape[0]

    @pl.when(pl.program_id(0) % tiles_per_seq == 0)
    def _():
        tbuf[0:A_CARRY, :] = jnp.zeros((A_CARRY, A_WIDTH), F32)
        ubuf[0:C_CARRY, :] = jnp.zeros((C_CARRY, C_WIDTH), F32)

    h = _rms(x_ref[...], g_ref[...]).astype(BF16)

    za = _dot(h, w_ref[:, 0:QKV_LO])
    a_h, a_b, a_c = za[:, 0:A_WIDTH], za[:, A_WIDTH:2 * A_WIDTH], za[:, 2 * A_WIDTH:3 * A_WIDTH]
    t = a_c * a_h
    tbuf[A_CARRY:A_CARRY + t_rows, :] = t
    caw = caw_ref[...]
    conv = (caw[0:1] * tbuf[A_CARRY - 2:A_CARRY - 2 + t_rows, :]
            + caw[1:2] * tbuf[A_CARRY - 1:A_CARRY - 1 + t_rows, :]
            + caw[2:3] * t)
    tbuf[0:A_CARRY, :] = tbuf[t_rows:t_rows + A_CARRY, :]
    ya_ref[...] = _rms(a_b * conv, ga_ref[...]).astype(BF16)

    zqkv = _dot(h, w_ref[:, QKV_LO:C_LO])
    qkv_ref[:, 0:B_WIDTH] = zqkv[:, 0:B_WIDTH] * (D_HEAD ** -0.5)
    qkv_ref[:, B_WIDTH:3 * B_WIDTH] = zqkv[:, B_WIDTH:3 * B_WIDTH]

    zc = _dot(h, w_ref[:, C_LO:IN_COLS])
    ubuf[C_CARRY:C_CARRY + t_rows, :] = zc[:, 0:C_WIDTH] * jax.nn.sigmoid(zc[:, C_WIDTH:2 * C_WIDTH])
    ccw = ccw_ref[...]
    first = C_CARRY - (CONFORMER_CONV - 1)
    shift_rows = C_CARRY + t_rows - SUBLANES
    for ph in range(1, SUBLANES):
        ushift[ph - 1, :, :] = ubuf[pl.ds(ph, shift_rows), :]
    for rc in range(t_rows // CONV_ROWS):
        acc = jnp.broadcast_to(ccb_ref[...], (CONV_ROWS, C_WIDTH))
        for k in range(CONFORMER_CONV):
            tiles, ph = divmod(first + k, SUBLANES)
            rows = pl.ds(rc * CONV_ROWS + tiles * SUBLANES, CONV_ROWS)
            acc = acc + ccw[k:k + 1] * (ubuf[rows, :] if ph == 0 else ushift[ph - 1, rows, :])
        mu = jnp.mean(acc, axis=-1, keepdims=True)
        cen = acc - mu
        var = jnp.mean(cen * cen, axis=-1, keepdims=True)
        ln = cen * lax.rsqrt(var + EPS) * lng_ref[...] + lnb_ref[...]
        yc = ln * jax.nn.sigmoid(ln)
        yc_ref[rc * CONV_ROWS:(rc + 1) * CONV_ROWS, :] = _rms(yc, gc_ref[...]).astype(BF16)
    ubuf[0:C_CARRY, :] = ubuf[t_rows:t_rows + C_CARRY, :]


def _inproj(x, layer, seq, g, w_in, caw, ccw, ccb, lng, lnb, ga, gc):
    n = x.shape[0]
    row = lambda width: pl.BlockSpec((None, 1, width), lambda i: (layer, 0, 0))
    return pl.pallas_call(
        functools.partial(_inproj_kernel, tiles_per_seq=seq // T_IN),
        grid=(n // T_IN,),
        in_specs=[pl.BlockSpec((T_IN, D_MODEL), lambda i: (i, 0)),
                  row(D_MODEL),
                  pl.BlockSpec((None, D_MODEL, IN_COLS), lambda i: (layer, 0, 0), pipeline_mode=pl.Buffered(1)),
                  pl.BlockSpec((None, SHORT_CONV, A_WIDTH), lambda i: (layer, 0, 0)),
                  pl.BlockSpec((None, CONFORMER_CONV, C_WIDTH), lambda i: (layer, 0, 0)),
                  row(C_WIDTH), row(C_WIDTH), row(C_WIDTH), row(A_WIDTH), row(C_WIDTH)],
        out_specs=[pl.BlockSpec((T_IN, A_WIDTH), lambda i: (i, 0)),
                   pl.BlockSpec((T_IN, 3 * B_WIDTH), lambda i: (i, 0)),
                   pl.BlockSpec((T_IN, C_WIDTH), lambda i: (i, 0))],
        out_shape=[jax.ShapeDtypeStruct((n, A_WIDTH), BF16),
                   jax.ShapeDtypeStruct((n, 3 * B_WIDTH), F32),
                   jax.ShapeDtypeStruct((n, C_WIDTH), BF16)],
        scratch_shapes=[pltpu.VMEM((A_CARRY + T_IN, A_WIDTH), F32),
                        pltpu.VMEM((C_CARRY + T_IN, C_WIDTH), F32),
                        pltpu.VMEM((SUBLANES - 1, C_CARRY + T_IN - SUBLANES, C_WIDTH), F32)],
        compiler_params=pltpu.CompilerParams(dimension_semantics=("arbitrary",), vmem_limit_bytes=VMEM_LIMIT),
        name="inproj_mix",
    )(x, g, w_in, caw, ccw, ccb, lng, lnb, ga, gc)


def _attn_kernel(q_ref, k_ref, v_ref, bias_ref, o_ref, g4q, g4k, g4v, q0s, q1s, ks, v0s, v1s, s_buf, p_buf, *stats):
    seq = q_ref.shape[0]
    n_blocks = seq // BLK
    quarter = seq // 4
    low = lax.broadcasted_iota(jnp.int32, (1, LANES), 1) < D_HEAD
    nt = (((1,), (1,)), ((), ()))
    q_masked = (q0s, q1s)
    v_ones = (v0s, v1s)

    def fill(dst, qv, kv, vv):
        q0s[dst, :] = jnp.where(low, qv, 0.0).astype(BF16)
        q1s[dst, :] = jnp.where(low, 0.0, qv).astype(BF16)
        ks[dst, :] = kv.astype(BF16)
        v0s[dst, :] = jnp.where(low, vv, 1.0).astype(BF16)
        v1s[dst, :] = jnp.where(low, 1.0, vv).astype(BF16)

    for br, d in enumerate(DILATIONS):
        if d == 1:
            for c in range(4):
                rows = pl.ds(c * quarter, quarter)
                fill(rows, q_ref[rows, :], k_ref[rows, :], v_ref[rows, :])
        elif d == 4:
            for r in range(4):
                src, dst = pl.ds(r, quarter, stride=4), pl.ds(r * quarter, quarter)
                qv, kv, vv = q_ref[src, :], k_ref[src, :], v_ref[src, :]
                g4q[dst, :], g4k[dst, :], g4v[dst, :] = qv, kv, vv
                fill(dst, qv, kv, vv)
        else:
            for r in range(16):
                src, dst = pl.ds((r % 4) * quarter + r // 4, BLK, stride=4), pl.ds(r * BLK, BLK)
                fill(dst, g4q[src, :], g4k[src, :], g4v[src, :])

        per_class = seq // d // BLK
        blocks = []
        for b in range(n_blocks):
            r, j = divmod(b, per_class)
            rows = pl.ds(b * BLK, BLK)
            keys = rows if j == 0 else pl.ds((b - 1) * BLK, 2 * BLK)
            cols = pl.ds(BLK, BLK) if j == 0 else pl.ds(0, 2 * BLK)
            out = pl.ds((r % 4) * quarter + r // 4, BLK, stride=4) if d == 16 else rows
            blocks.append((b, rows, keys, cols, out))

        for b, rows, keys, cols, out in blocks:
            for hh in range(2):
                s_buf[hh, b, :, cols] = (lax.dot_general(q_masked[hh][rows, :], ks[keys, :], nt,
                                                         preferred_element_type=F32) + bias_ref[br, hh, :, cols])
        for b, rows, keys, cols, out in blocks:
            for hh in range(2):
                s = s_buf[hh, b, :, cols]
                m = jnp.max(s, axis=-1, keepdims=True)
                p_buf[hh, b, :, cols] = jnp.exp(s - m).astype(BF16)
                stats[2 * (2 * br + hh) + 1][out, :] = jnp.broadcast_to(m, (BLK, LANES))
        for b, rows, keys, cols, out in blocks:
            for hh in range(2):
                stats[2 * (2 * br + hh)][out, :] = _dot(p_buf[hh, b, :, cols], v_ones[hh][keys, :])

    for t in range(n_blocks):
        r4, n = divmod(t, 4)
        rows4 = pl.ds(t * BLK, BLK)
        nat = pl.ds(4 * BLK * n + r4, BLK, stride=4)
        y = []
        for hh in range(2):
            where = (nat, rows4, rows4)
            ms = [stats[2 * (2 * br + hh) + 1][where[br], :] for br in range(len(DILATIONS))]
            m_all = jnp.maximum(jnp.maximum(ms[0], ms[1]), ms[2])
            tot = jnp.zeros((BLK, LANES), F32)
            for br in range(len(DILATIONS)):
                tot = tot + jnp.exp(ms[br] - m_all) * stats[2 * (2 * br + hh)][where[br], :]
            y.append(tot / pltpu.roll(tot, D_HEAD, axis=1))
        o_ref[nat, :] = jnp.where(low, y[0], y[1])


def _attention(qkv, bias_tab, batch, seq):
    qkv3 = qkv.reshape(batch, seq, 3 * B_WIDTH)
    pairs = B_WIDTH // LANES
    n_br = len(DILATIONS)
    blk = lambda off: pl.BlockSpec((None, seq, LANES), lambda g, b: (b, 0, off * pairs + g))
    return pl.pallas_call(
        _attn_kernel,
        grid=(pairs, batch),
        in_specs=[blk(0), blk(1), blk(2),
                  pl.BlockSpec((n_br, 2, BLK, 2 * BLK), lambda g, b: (0, g, 0, 0))],
        out_specs=pl.BlockSpec((None, seq, LANES), lambda g, b: (b, 0, g)),
        out_shape=jax.ShapeDtypeStruct((batch, seq, B_WIDTH), F32),
        scratch_shapes=([pltpu.VMEM((seq, LANES), F32)] * 3 + [pltpu.VMEM((seq, LANES), BF16)] * 5
                        + [pltpu.VMEM((2, seq // BLK, BLK, 2 * BLK), F32),
                           pltpu.VMEM((2, seq // BLK, BLK, 2 * BLK), BF16)]
                        + [pltpu.VMEM((seq, LANES), F32)] * (4 * n_br)),
        compiler_params=pltpu.CompilerParams(dimension_semantics=("arbitrary", "arbitrary"),
                                             vmem_limit_bytes=VMEM_LIMIT),
        name="dilated_attn",
    )(qkv3, qkv3, qkv3, bias_tab)


def _ffn_kernel(x_ref, ya_ref, yb_ref, yc_ref, gb_ref, wout_ref, gf_ref, wup_ref, cfw_ref, wdn_ref, fg_ref,
                o_ref, ubuf, act, *, tiles_per_seq, final):
    t_rows = x_ref.shape[0]

    @pl.when(pl.program_id(0) % tiles_per_seq == 0)
    def _():
        ubuf[0:A_CARRY, :] = jnp.zeros((A_CARRY, 2 * D_FF), F32)

    y = jnp.concatenate([ya_ref[...], _rms(yb_ref[...], gb_ref[...]).astype(BF16), yc_ref[...]], axis=-1)
    x1 = x_ref[...] + _dot(y, wout_ref[...])
    h = _rms(x1, gf_ref[...]).astype(BF16)

    def conv3(c0):
        cols = pl.ds(c0, FF_CHUNK)
        up = _dot(h, wup_ref[:, cols])
        ubuf[A_CARRY:A_CARRY + t_rows, cols] = up
        w = cfw_ref[:, cols]
        out = (w[0:1] * ubuf[A_CARRY - 2:A_CARRY - 2 + t_rows, cols]
               + w[1:2] * ubuf[A_CARRY - 1:A_CARRY - 1 + t_rows, cols]
               + w[2:3] * up)
        ubuf[0:A_CARRY, cols] = ubuf[t_rows:t_rows + A_CARRY, cols]
        return out

    for c in range(D_FF // FF_CHUNK):
        gate = conv3(c * FF_CHUNK)
        val = conv3(D_FF + c * FF_CHUNK)
        act[:, c * FF_CHUNK:(c + 1) * FF_CHUNK] = (gate * jax.nn.sigmoid(gate) * val).astype(BF16)

    x2 = x1 + _dot(act[...], wdn_ref[...])
    if final:
        x2 = _rms(x2, fg_ref[...])
    o_ref[...] = x2


def _ffn(x, ya, yb, yc, layer, seq, final, gb, w_out, gf, w_up, cfw, w_dn, fg):
    n = x.shape[0]
    tile = lambda width: pl.BlockSpec((T_FFN, width), lambda i: (i, 0))
    row = lambda width: pl.BlockSpec((None, 1, width), lambda i: (layer, 0, 0))
    weight = lambda r, c: pl.BlockSpec((None, r, c), lambda i: (layer, 0, 0), pipeline_mode=pl.Buffered(1))
    return pl.pallas_call(
        functools.partial(_ffn_kernel, tiles_per_seq=seq // T_FFN, final=final),
        grid=(n // T_FFN,),
        in_specs=[tile(D_MODEL), tile(A_WIDTH), tile(B_WIDTH), tile(C_WIDTH),
                  row(B_WIDTH), weight(D_MODEL, D_MODEL), row(D_MODEL), weight(D_MODEL, 2 * D_FF),
                  pl.BlockSpec((None, FFN_CONV, 2 * D_FF), lambda i: (layer, 0, 0)),
                  weight(D_FF, D_MODEL),
                  pl.BlockSpec((1, D_MODEL), lambda i: (0, 0))],
        out_specs=tile(D_MODEL),
        out_shape=jax.ShapeDtypeStruct((n, D_MODEL), F32),
        scratch_shapes=[pltpu.VMEM((A_CARRY + T_FFN, 2 * D_FF), F32),
                        pltpu.VMEM((T_FFN, D_FF), BF16)],
        compiler_params=pltpu.CompilerParams(dimension_semantics=("arbitrary",), vmem_limit_bytes=VMEM_LIMIT),
        name="outproj_ffn",
    )(x, ya, yb, yc, gb, w_out, gf, w_up, cfw, w_dn, fg)


def kernel(x, norm_mix_g, w_in, conv_a_w, conv_c_w, conv_c_b, ln_c_g, ln_c_b, out_norm_g, w_out, norm_ffn_g, w_up, conv_f_w, w_down, rel_bias, final_g):
    batch, seq, d_model = x.shape
    depth = w_in.shape[0]
    assert d_model == D_MODEL and seq % T_IN == 0 and seq % T_FFN == 0 and seq % (BLK * max(DILATIONS)) == 0
    assert w_in.shape[2] == IN_COLS and w_up.shape[2] == 2 * D_FF and D_FF % FF_CHUNK == 0

    rows = lambda a: a.reshape(depth, 1, a.shape[-1])
    w_in_b, w_out_b, w_up_b, w_dn_b = (w.astype(BF16) for w in (w_in, w_out, w_up, w_down))
    g_mix, g_ffn, ccb, lng, lnb = (rows(a) for a in (norm_mix_g, norm_ffn_g, conv_c_b, ln_c_g, ln_c_b))
    g_a = rows(out_norm_g[:, 0:A_WIDTH])
    g_b = rows(out_norm_g[:, A_WIDTH:A_WIDTH + B_WIDTH])
    g_c = rows(out_norm_g[:, A_WIDTH + B_WIDTH:])
    fg = final_g.reshape(1, D_MODEL)

    bias_tab = _bias_table(rel_bias)
    xf = x.reshape(batch * seq, D_MODEL)
    for layer in range(depth):
        ya, qkv, yc = _inproj(xf, layer, seq, g_mix, w_in_b, conv_a_w, conv_c_w, ccb, lng, lnb, g_a, g_c)
        yb = _attention(qkv, bias_tab, batch, seq).reshape(batch * seq, B_WIDTH)
        xf = _ffn(xf, ya, yb, yc, layer, seq, layer == depth - 1, g_b, w_out_b, g_ffn, w_up_b, conv_f_w, w_dn_b, fg)
    return xf.reshape(batch, seq, D_MODEL)
```

```python
import functools
import math

import numpy as np
import jax
import jax.numpy as jnp
from jax import lax
from jax.experimental import pallas as pl
from jax.experimental.pallas import tpu as pltpu

F32 = jnp.float32
BF16 = jnp.bfloat16

D_MODEL = 1024
D_HEAD = 64
A_WIDTH = 256
B_HEADS = 8
B_WIDTH = 512
C_WIDTH = 256
IN_COLS = 3 * A_WIDTH + 3 * B_WIDTH + 2 * C_WIDTH
QKV_LO = 3 * A_WIDTH
C_LO = QKV_LO + 3 * B_WIDTH
DILATIONS = (1, 4, 16)
N_KEYS = 128
BLK = 128
NUM_BUCKETS = 32
MAX_DISTANCE = 2048
SHORT_CONV = 3
CONFORMER_CONV = 31
FFN_CONV = 3
D_FF = 2816
EPS = 1e-6
NEG = -1e30
LOG2E = math.log2(math.e)

LANES = 128
SUBLANES = 8
VMEM_LIMIT = 56 * 1024 * 1024

T_IN = 512
T_FFN = 512
CONV_ROWS = 64
FF_CHUNK = 256
A_CARRY = SUBLANES
C_CARRY = 32


def _rms(x, g):
    return x * lax.rsqrt(jnp.mean(x * x, axis=-1, keepdims=True) + EPS) * g


def _dot(a, b):
    return jnp.dot(a, b, preferred_element_type=F32)


def _bucket_table():
    rel = np.arange(BLK)[:, None] - np.arange(2 * BLK)[None, :] + BLK
    valid = (rel >= 0) & (rel <= N_KEYS)
    max_exact = NUM_BUCKETS // 2
    out = []
    for d in DILATIONS:
        dist = np.maximum(rel, 0) * d
        d_f = np.maximum(dist, 1).astype(np.float32)
        large = max_exact + (np.log(d_f / np.float32(max_exact)) / np.float32(math.log(MAX_DISTANCE / max_exact))
                             * np.float32(NUM_BUCKETS - max_exact)).astype(np.int32)
        large = np.minimum(large, NUM_BUCKETS - 1)
        bucket = np.where(dist < max_exact, dist, large)
        out.append(np.where(valid, bucket, -1).astype(np.int32))
    return np.stack(out)


def _bias_kernel(rb_ref, bucket_ref, o_ref):
    h = pl.program_id(1)
    bucket = bucket_ref[...]
    acc = jnp.full(bucket.shape, NEG, F32)
    for b in range(NUM_BUCKETS):
        acc = jnp.where(bucket == b, rb_ref[b, h] * LOG2E, acc)
    o_ref[...] = acc


def _bias_table(rel_bias):
    buckets = jnp.asarray(_bucket_table())
    return pl.pallas_call(
        _bias_kernel,
        grid=(len(DILATIONS), B_HEADS),
        in_specs=[pl.BlockSpec(memory_space=pltpu.SMEM),
                  pl.BlockSpec((None, BLK, 2 * BLK), lambda br, h: (br, 0, 0))],
        out_specs=pl.BlockSpec((None, None, BLK, 2 * BLK), lambda br, h: (br, h, 0, 0)),
        out_shape=jax.ShapeDtypeStruct((len(DILATIONS), B_HEADS, BLK, 2 * BLK), F32),
        name="bias_table",
    )(rel_bias, buckets)


def _inproj_kernel(x_ref, g_ref, w_ref, caw_ref, ccw_ref, ccb_ref, lng_ref, lnb_ref, ga_ref, gc_ref,
                   ya_ref, qkv_ref, yc_ref, tbuf, ubuf, ushift, *, tiles_per_seq):
    t_rows = x_ref.shape[0]

    @pl.when(pl.program_id(0) % tiles_per_seq == 0)
    def _():
        tbuf[0:A_CARRY, :] = jnp.zeros((A_CARRY, A_WIDTH), F32)
        ubuf[0:C_CARRY, :] = jnp.zeros((C_CARRY, C_WIDTH), F32)

    h = _rms(x_ref[...], g_ref[...]).astype(BF16)

    zc = _dot(h, w_ref[:, C_LO:IN_COLS])
    ubuf[C_CARRY:C_CARRY + t_rows, :] = zc[:, 0:C_WIDTH] * jax.nn.sigmoid(zc[:, C_WIDTH:2 * C_WIDTH])
    ccw = ccw_ref[...]
    first = C_CARRY - (CONFORMER_CONV - 1)
    shift_rows = C_CARRY + t_rows - SUBLANES
    for ph in range(1, SUBLANES):
        ushift[ph - 1, :, :] = ubuf[pl.ds(ph, shift_rows), :]
    for rc in range(t_rows // CONV_ROWS):
        acc = jnp.broadcast_to(ccb_ref[...], (CONV_ROWS, C_WIDTH))
        for k in range(CONFORMER_CONV):
            tiles, ph = divmod(first + k, SUBLANES)
            rows = pl.ds(rc * CONV_ROWS + tiles * SUBLANES, CONV_ROWS)
            acc = acc + ccw[k:k + 1] * (ubuf[rows, :] if ph == 0 else ushift[ph - 1, rows, :])
        mu = jnp.mean(acc, axis=-1, keepdims=True)
        cen = acc - mu
        var = jnp.mean(cen * cen, axis=-1, keepdims=True)
        ln = cen * lax.rsqrt(var + EPS) * lng_ref[...] + lnb_ref[...]
        yc = ln * jax.nn.sigmoid(ln)
        yc_ref[rc * CONV_ROWS:(rc + 1) * CONV_ROWS, :] = _rms(yc, gc_ref[...]).astype(BF16)
    ubuf[0:C_CARRY, :] = ubuf[t_rows:t_rows + C_CARRY, :]

    za = _dot(h, w_ref[:, 0:QKV_LO])
    a_h, a_b, a_c = za[:, 0:A_WIDTH], za[:, A_WIDTH:2 * A_WIDTH], za[:, 2 * A_WIDTH:3 * A_WIDTH]
    t = a_c * a_h
    tbuf[A_CARRY:A_CARRY + t_rows, :] = t
    caw = caw_ref[...]
    conv = (caw[0:1] * tbuf[A_CARRY - 2:A_CARRY - 2 + t_rows, :]
            + caw[1:2] * tbuf[A_CARRY - 1:A_CARRY - 1 + t_rows, :]
            + caw[2:3] * t)
    tbuf[0:A_CARRY, :] = tbuf[t_rows:t_rows + A_CARRY, :]
    ya_ref[...] = _rms(a_b * conv, ga_ref[...]).astype(BF16)

    zqkv = _dot(h, w_ref[:, QKV_LO:C_LO])
    qkv_ref[:, 0:B_WIDTH] = zqkv[:, 0:B_WIDTH] * (D_HEAD ** -0.5 * LOG2E)
    qkv_ref[:, B_WIDTH:3 * B_WIDTH] = zqkv[:, B_WIDTH:3 * B_WIDTH]


def _inproj(x, layer, seq, g, w_in, caw, ccw, ccb, lng, lnb, ga, gc):
    n = x.shape[0]
    row = lambda width: pl.BlockSpec((None, 1, width), lambda i: (layer, 0, 0))
    return pl.pallas_call(
        functools.partial(_inproj_kernel, tiles_per_seq=seq // T_IN),
        grid=(n // T_IN,),
        in_specs=[pl.BlockSpec((T_IN, D_MODEL), lambda i: (i, 0)),
                  row(D_MODEL),
                  pl.BlockSpec((None, D_MODEL, IN_COLS), lambda i: (layer, 0, 0), pipeline_mode=pl.Buffered(1)),
                  pl.BlockSpec((None, SHORT_CONV, A_WIDTH), lambda i: (layer, 0, 0)),
                  pl.BlockSpec((None, CONFORMER_CONV, C_WIDTH), lambda i: (layer, 0, 0)),
                  row(C_WIDTH), row(C_WIDTH), row(C_WIDTH), row(A_WIDTH), row(C_WIDTH)],
        out_specs=[pl.BlockSpec((T_IN, A_WIDTH), lambda i: (i, 0)),
                   pl.BlockSpec((T_IN, 3 * B_WIDTH), lambda i: (i, 0)),
                   pl.BlockSpec((T_IN, C_WIDTH), lambda i: (i, 0))],
        out_shape=[jax.ShapeDtypeStruct((n, A_WIDTH), BF16),
                   jax.ShapeDtypeStruct((n, 3 * B_WIDTH), F32),
                   jax.ShapeDtypeStruct((n, C_WIDTH), BF16)],
        scratch_shapes=[pltpu.VMEM((A_CARRY + T_IN, A_WIDTH), F32),
                        pltpu.VMEM((C_CARRY + T_IN, C_WIDTH), F32),
                        pltpu.VMEM((SUBLANES - 1, C_CARRY + T_IN - SUBLANES, C_WIDTH), F32)],
        compiler_params=pltpu.CompilerParams(dimension_semantics=("arbitrary",), vmem_limit_bytes=VMEM_LIMIT),
        name="inproj_mix",
    )(x, g, w_in, caw, ccw, ccb, lng, lnb, ga, gc)


def _attn_kernel(q_ref, k_ref, v_ref, bias_ref, o_ref, g4q, g4k, g4v, qs, ks, v0s, v1s, s_buf, p_buf, *stats):
    seq = q_ref.shape[0]
    n_blocks = seq // BLK
    quarter = seq // 4
    low = lax.broadcasted_iota(jnp.int32, (1, LANES), 1) < D_HEAD
    nt = (((1,), (1,)), ((), ()))
    head_lanes = (low, jnp.logical_not(low))
    v_ones = (v0s, v1s)

    def fill(dst, qv, kv, vv):
        qs[dst, :] = qv.astype(BF16)
        ks[dst, :] = kv.astype(BF16)
        v0s[dst, :] = jnp.where(low, vv, 1.0).astype(BF16)
        v1s[dst, :] = jnp.where(low, 1.0, vv).astype(BF16)

    for br, d in enumerate(DILATIONS):
        if d == 1:
            for c in range(4):
                rows = pl.ds(c * quarter, quarter)
                fill(rows, q_ref[rows, :], k_ref[rows, :], v_ref[rows, :])
        elif d == 4:
            for r in range(4):
                src, dst = pl.ds(r, quarter, stride=4), pl.ds(r * quarter, quarter)
                qv, kv, vv = q_ref[src, :], k_ref[src, :], v_ref[src, :]
                g4q[dst, :], g4k[dst, :], g4v[dst, :] = qv, kv, vv
                fill(dst, qv, kv, vv)
        else:
            for r in range(16):
                src, dst = pl.ds((r % 4) * quarter + r // 4, BLK, stride=4), pl.ds(r * BLK, BLK)
                fill(dst, g4q[src, :], g4k[src, :], g4v[src, :])

        per_class = seq // d // BLK
        blocks = []
        for b in range(n_blocks):
            r, j = divmod(b, per_class)
            rows = pl.ds(b * BLK, BLK)
            keys = rows if j == 0 else pl.ds((b - 1) * BLK, 2 * BLK)
            cols = pl.ds(BLK, BLK) if j == 0 else pl.ds(0, 2 * BLK)
            out = pl.ds((r % 4) * quarter + r // 4, BLK, stride=4) if d == 16 else rows
            blocks.append((b, rows, keys, cols, out))

        for b, rows, keys, cols, out in blocks:
            for hh in range(2):
                q_head = jnp.where(head_lanes[hh], qs[rows, :], jnp.zeros((), BF16))
                s_buf[hh, b, :, cols] = (lax.dot_general(q_head, ks[keys, :], nt, preferred_element_type=F32)
                                         + bias_ref[br, hh, :, cols])
        for b, rows, keys, cols, out in blocks:
            for hh in range(2):
                s = s_buf[hh, b, :, cols]
                m = jnp.max(s, axis=-1, keepdims=True)
                p_buf[hh, b, :, cols] = jnp.exp2(s - m).astype(BF16)
                stats[2 * (2 * br + hh) + 1][out, :] = jnp.broadcast_to(m, (BLK, LANES))
        for b, rows, keys, cols, out in blocks:
            for hh in range(2):
                stats[2 * (2 * br + hh)][out, :] = _dot(p_buf[hh, b, :, cols], v_ones[hh][keys, :])

    for t in range(n_blocks):
        r4, n = divmod(t, 4)
        rows4 = pl.ds(t * BLK, BLK)
        nat = pl.ds(4 * BLK * n + r4, BLK, stride=4)
        y = []
        for hh in range(2):
            where = (nat, rows4, rows4)
            ms = [stats[2 * (2 * br + hh) + 1][where[br], :] for br in range(len(DILATIONS))]
            m_all = jnp.maximum(jnp.maximum(ms[0], ms[1]), ms[2])
            tot = jnp.zeros((BLK, LANES), F32)
            for br in range(len(DILATIONS)):
                tot = tot + jnp.exp2(ms[br] - m_all) * stats[2 * (2 * br + hh)][where[br], :]
            y.append(tot / pltpu.roll(tot, D_HEAD, axis=1))
        o_ref[nat, :] = jnp.where(low, y[0], y[1])


def _attention(qkv, bias_tab, batch, seq):
    qkv3 = qkv.reshape(batch, seq, 3 * B_WIDTH)
    pairs = B_WIDTH // LANES
    n_br = len(DILATIONS)
    blk = lambda off: pl.BlockSpec((None, seq, LANES), lambda g, b: (b, 0, off * pairs + g))
    return pl.pallas_call(
        _attn_kernel,
        grid=(pairs, batch),
        in_specs=[blk(0), blk(1), blk(2),
                  pl.BlockSpec((n_br, 2, BLK, 2 * BLK), lambda g, b: (0, g, 0, 0))],
        out_specs=pl.BlockSpec((None, seq, LANES), lambda g, b: (b, 0, g)),
        out_shape=jax.ShapeDtypeStruct((batch, seq, B_WIDTH), F32),
        scratch_shapes=([pltpu.VMEM((seq, LANES), F32)] * 3 + [pltpu.VMEM((seq, LANES), BF16)] * 4
                        + [pltpu.VMEM((2, seq // BLK, BLK, 2 * BLK), F32),
                           pltpu.VMEM((2, seq // BLK, BLK, 2 * BLK), BF16)]
                        + [pltpu.VMEM((seq, LANES), F32)] * (4 * n_br)),
        compiler_params=pltpu.CompilerParams(dimension_semantics=("arbitrary", "arbitrary"),
                                             vmem_limit_bytes=VMEM_LIMIT),
        name="dilated_attn",
    )(qkv3, qkv3, qkv3, bias_tab)


def _ffn_kernel(x_ref, ya_ref, yb_ref, yc_ref, gb_ref, wout_ref, gf_ref, wup_ref, cfw_ref, wdn_ref, fg_ref,
                o_ref, ubuf, act, *, tiles_per_seq, final):
    t_rows = x_ref.shape[0]

    @pl.when(pl.program_id(0) % tiles_per_seq == 0)
    def _():
        ubuf[0:A_CARRY, :] = jnp.zeros((A_CARRY, 2 * D_FF), F32)

    y = jnp.concatenate([ya_ref[...], _rms(yb_ref[...], gb_ref[...]).astype(BF16), yc_ref[...]], axis=-1)
    x1 = x_ref[...] + _dot(y, wout_ref[...])
    h = _rms(x1, gf_ref[...]).astype(BF16)

    def conv3(c0):
        cols = pl.ds(c0, FF_CHUNK)
        up = _dot(h, wup_ref[:, cols])
        ubuf[A_CARRY:A_CARRY + t_rows, cols] = up
        w = cfw_ref[:, cols]
        out = (w[0:1] * ubuf[A_CARRY - 2:A_CARRY - 2 + t_rows, cols]
               + w[1:2] * ubuf[A_CARRY - 1:A_CARRY - 1 + t_rows, cols]
               + w[2:3] * up)
        ubuf[0:A_CARRY, cols] = ubuf[t_rows:t_rows + A_CARRY, cols]
        return out

    for c in range(D_FF // FF_CHUNK):
        gate = conv3(c * FF_CHUNK)
        val = conv3(D_FF + c * FF_CHUNK)
        act[:, c * FF_CHUNK:(c + 1) * FF_CHUNK] = (gate * jax.nn.sigmoid(gate) * val).astype(BF16)

    x2 = x1 + _dot(act[...], wdn_ref[...])
    if final:
        x2 = _rms(x2, fg_ref[...])
    o_ref[...] = x2


def _ffn(x, ya, yb, yc, layer, seq, final, gb, w_out, gf, w_up, cfw, w_dn, fg):
    n = x.shape[0]
    tile = lambda width: pl.BlockSpec((T_FFN, width), lambda i: (i, 0))
    row = lambda width: pl.BlockSpec((None, 1, width), lambda i: (layer, 0, 0))
    weight = lambda r, c: pl.BlockSpec((None, r, c), lambda i: (layer, 0, 0), pipeline_mode=pl.Buffered(1))
    return pl.pallas_call(
        functools.partial(_ffn_kernel, tiles_per_seq=seq // T_FFN, final=final),
        grid=(n // T_FFN,),
        in_specs=[tile(D_MODEL), tile(A_WIDTH), tile(B_WIDTH), tile(C_WIDTH),
                  row(B_WIDTH), weight(D_MODEL, D_MODEL), row(D_MODEL), weight(D_MODEL, 2 * D_FF),
                  pl.BlockSpec((None, FFN_CONV, 2 * D_FF), lambda i: (layer, 0, 0)),
                  weight(D_FF, D_MODEL),
                  pl.BlockSpec((1, D_MODEL), lambda i: (0, 0))],
        out_specs=tile(D_MODEL),
        out_shape=jax.ShapeDtypeStruct((n, D_MODEL), F32),
        scratch_shapes=[pltpu.VMEM((A_CARRY + T_FFN, 2 * D_FF), F32),
                        pltpu.VMEM((T_FFN, D_FF), BF16)],
        compiler_params=pltpu.CompilerParams(dimension_semantics=("arbitrary",), vmem_limit_bytes=VMEM_LIMIT),
        name="outproj_ffn",
    )(x, ya, yb, yc, gb, w_out, gf, w_up, cfw, w_dn, fg)


def kernel(x, norm_mix_g, w_in, conv_a_w, conv_c_w, conv_c_b, ln_c_g, ln_c_b, out_norm_g, w_out, norm_ffn_g, w_up, conv_f_w, w_down, rel_bias, final_g):
    batch, seq, d_model = x.shape
    depth = w_in.shape[0]
    assert d_model == D_MODEL and seq % T_IN == 0 and seq % T_FFN == 0 and seq % (BLK * max(DILATIONS)) == 0
    assert w_in.shape[2] == IN_COLS and w_up.shape[2] == 2 * D_FF and D_FF % FF_CHUNK == 0

    rows = lambda a: a.reshape(depth, 1, a.shape[-1])
    w_in_b, w_out_b, w_up_b, w_dn_b = (w.astype(BF16) for w in (w_in, w_out, w_up, w_down))
    g_mix, g_ffn, ccb, lng, lnb = (rows(a) for a in (norm_mix_g, norm_ffn_g, conv_c_b, ln_c_g, ln_c_b))
    g_a = rows(out_norm_g[:, 0:A_WIDTH])
    g_b = rows(out_norm_g[:, A_WIDTH:A_WIDTH + B_WIDTH])
    g_c = rows(out_norm_g[:, A_WIDTH + B_WIDTH:])
    fg = final_g.reshape(1, D_MODEL)

    bias_tab = _bias_table(rel_bias)
    xf = x.reshape(batch * seq, D_MODEL)
    for layer in range(depth):
        ya, qkv, yc = _inproj(xf, layer, seq, g_mix, w_in_b, conv_a_w, conv_c_w, ccb, lng, lnb, g_a, g_c)
        yb = _attention(qkv, bias_tab, batch, seq).reshape(batch * seq, B_WIDTH)
        xf = _ffn(xf, ya, yb, yc, layer, seq, layer == depth - 1, g_b, w_out_b, g_ffn, w_up_b, conv_f_w, w_dn_b, fg)
    return xf.reshape(batch, seq, D_MODEL)
```

```python
import functools
import math

import numpy as np
import jax
import jax.numpy as jnp
from jax import lax
from jax.experimental import pallas as pl
from jax.experimental.pallas import tpu as pltpu

F32 = jnp.float32
BF16 = jnp.bfloat16

D_MODEL = 1024
D_HEAD = 64
A_WIDTH = 256
B_HEADS = 8
B_WIDTH = 512
C_WIDTH = 256
IN_COLS = 3 * A_WIDTH + 3 * B_WIDTH + 2 * C_WIDTH
QKV_LO = 3 * A_WIDTH
C_LO = QKV_LO + 3 * B_WIDTH
DILATIONS = (1, 4, 16)
N_KEYS = 128
BLK = 128
NUM_BUCKETS = 32
MAX_DISTANCE = 2048
SHORT_CONV = 3
CONFORMER_CONV = 31
FFN_CONV = 3
D_FF = 2816
EPS = 1e-6
NEG = -1e30
LOG2E = math.log2(math.e)

LANES = 128
SUBLANES = 8
VMEM_LIMIT = 56 * 1024 * 1024

T_IN = 512
T_FFN = 512
CONV_ROWS = 64
FF_CHUNK = 256
A_CARRY = SUBLANES
C_CARRY = 32


def _rms(x, g):
    return x * lax.rsqrt(jnp.mean(x * x, axis=-1, keepdims=True) + EPS) * g


def _dot(a, b):
    return jnp.dot(a, b, preferred_element_type=F32)


def _bucket_table():
    rel = np.arange(BLK)[:, None] - np.arange(2 * BLK)[None, :] + BLK
    valid = (rel >= 0) & (rel <= N_KEYS)
    max_exact = NUM_BUCKETS // 2
    out = []
    for d in DILATIONS:
        dist = np.maximum(rel, 0) * d
        d_f = np.maximum(dist, 1).astype(np.float32)
        large = max_exact + (np.log(d_f / np.float32(max_exact)) / np.float32(math.log(MAX_DISTANCE / max_exact))
                             * np.float32(NUM_BUCKETS - max_exact)).astype(np.int32)
        large = np.minimum(large, NUM_BUCKETS - 1)
        bucket = np.where(dist < max_exact, dist, large)
        out.append(np.where(valid, bucket, -1).astype(np.int32))
    return np.stack(out)


def _bias_kernel(rb_ref, bucket_ref, o_ref):
    h = pl.program_id(1)
    bucket = bucket_ref[...]
    acc = jnp.full(bucket.shape, NEG, F32)
    for b in range(NUM_BUCKETS):
        acc = jnp.where(bucket == b, rb_ref[b, h] * LOG2E, acc)
    o_ref[...] = acc


def _bias_table(rel_bias):
    buckets = jnp.asarray(_bucket_table())
    return pl.pallas_call(
        _bias_kernel,
        grid=(len(DILATIONS), B_HEADS),
        in_specs=[pl.BlockSpec(memory_space=pltpu.SMEM),
                  pl.BlockSpec((None, BLK, 2 * BLK), lambda br, h: (br, 0, 0))],
        out_specs=pl.BlockSpec((None, None, BLK, 2 * BLK), lambda br, h: (br, h, 0, 0)),
        out_shape=jax.ShapeDtypeStruct((len(DILATIONS), B_HEADS, BLK, 2 * BLK), F32),
        name="bias_table",
    )(rel_bias, buckets)


def _inproj_kernel(x_ref, g_ref, w_ref, caw_ref, ccw_ref, ccb_ref, lng_ref, lnb_ref, ga_ref, gc_ref,
                   ya_ref, qkv_ref, yc_ref, tbuf, ubuf, ushift, *, tiles_per_seq):
    t_rows = x_ref.shape[0]

    @pl.when(pl.program_id(0) % tiles_per_seq == 0)
    def _():
        tbuf[0:A_CARRY, :] = jnp.zeros((A_CARRY, A_WIDTH), F32)
        ubuf[0:C_CARRY, :] = jnp.zeros((C_CARRY, C_WIDTH), F32)

    h = _rms(x_ref[...], g_ref[...]).astype(BF16)

    zc = _dot(h, w_ref[:, C_LO:IN_COLS])
    ubuf[C_CARRY:C_CARRY + t_rows, :] = zc[:, 0:C_WIDTH] * jax.nn.sigmoid(zc[:, C_WIDTH:2 * C_WIDTH])
    ccw = ccw_ref[...]
    first = C_CARRY - (CONFORMER_CONV - 1)
    shift_rows = C_CARRY + t_rows - SUBLANES
    for ph in range(1, SUBLANES):
        ushift[ph - 1, :, :] = ubuf[pl.ds(ph, shift_rows), :]
    for rc in range(t_rows // CONV_ROWS):
        acc = jnp.broadcast_to(ccb_ref[...], (CONV_ROWS, C_WIDTH))
        for k in range(CONFORMER_CONV):
            tiles, ph = divmod(first + k, SUBLANES)
            rows = pl.ds(rc * CONV_ROWS + tiles * SUBLANES, CONV_ROWS)
            acc = acc + ccw[k:k + 1] * (ubuf[rows, :] if ph == 0 else ushift[ph - 1, rows, :])
        mu = jnp.mean(acc, axis=-1, keepdims=True)
        cen = acc - mu
        var = jnp.mean(cen * cen, axis=-1, keepdims=True)
        ln = cen * lax.rsqrt(var + EPS) * lng_ref[...] + lnb_ref[...]
        yc = ln * jax.nn.sigmoid(ln)
        yc_ref[rc * CONV_ROWS:(rc + 1) * CONV_ROWS, :] = _rms(yc, gc_ref[...]).astype(BF16)
    ubuf[0:C_CARRY, :] = ubuf[t_rows:t_rows + C_CARRY, :]

    za = _dot(h, w_ref[:, 0:QKV_LO])
    a_h, a_b, a_c = za[:, 0:A_WIDTH], za[:, A_WIDTH:2 * A_WIDTH], za[:, 2 * A_WIDTH:3 * A_WIDTH]
    t = a_c * a_h
    tbuf[A_CARRY:A_CARRY + t_rows, :] = t
    caw = caw_ref[...]
    conv = (caw[0:1] * tbuf[A_CARRY - 2:A_CARRY - 2 + t_rows, :]
            + caw[1:2] * tbuf[A_CARRY - 1:A_CARRY - 1 + t_rows, :]
            + caw[2:3] * t)
    tbuf[0:A_CARRY, :] = tbuf[t_rows:t_rows + A_CARRY, :]
    ya_ref[...] = _rms(a_b * conv, ga_ref[...]).astype(BF16)

    zqkv = _dot(h, w_ref[:, QKV_LO:C_LO])
    qkv_ref[:, 0:B_WIDTH] = zqkv[:, 0:B_WIDTH] * (D_HEAD ** -0.5 * LOG2E)
    qkv_ref[:, B_WIDTH:3 * B_WIDTH] = zqkv[:, B_WIDTH:3 * B_WIDTH]


def _inproj(x, layer, seq, g, w_in, caw, ccw, ccb, lng, lnb, ga, gc):
    n = x.shape[0]
    row = lambda width: pl.BlockSpec((None, 1, width), lambda i: (layer, 0, 0))
    return pl.pallas_call(
        functools.partial(_inproj_kernel, tiles_per_seq=seq // T_IN),
        grid=(n // T_IN,),
        in_specs=[pl.BlockSpec((T_IN, D_MODEL), lambda i: (i, 0)),
                  row(D_MODEL),
                  pl.BlockSpec((None, D_MODEL, IN_COLS), lambda i: (layer, 0, 0), pipeline_mode=pl.Buffered(1)),
                  pl.BlockSpec((None, SHORT_CONV, A_WIDTH), lambda i: (layer, 0, 0)),
                  pl.BlockSpec((None, CONFORMER_CONV, C_WIDTH), lambda i: (layer, 0, 0)),
                  row(C_WIDTH), row(C_WIDTH), row(C_WIDTH), row(A_WIDTH), row(C_WIDTH)],
        out_specs=[pl.BlockSpec((T_IN, A_WIDTH), lambda i: (i, 0)),
                   pl.BlockSpec((T_IN, 3 * B_WIDTH), lambda i: (i, 0)),
                   pl.BlockSpec((T_IN, C_WIDTH), lambda i: (i, 0))],
        out_shape=[jax.ShapeDtypeStruct((n, A_WIDTH), BF16),
                   jax.ShapeDtypeStruct((n, 3 * B_WIDTH), F32),
                   jax.ShapeDtypeStruct((n, C_WIDTH), BF16)],
        scratch_shapes=[pltpu.VMEM((A_CARRY + T_IN, A_WIDTH), F32),
                        pltpu.VMEM((C_CARRY + T_IN, C_WIDTH), F32),
                        pltpu.VMEM((SUBLANES - 1, C_CARRY + T_IN - SUBLANES, C_WIDTH), F32)],
        compiler_params=pltpu.CompilerParams(dimension_semantics=("arbitrary",), vmem_limit_bytes=VMEM_LIMIT),
        name="inproj_mix",
    )(x, g, w_in, caw, ccw, ccb, lng, lnb, ga, gc)


def _attn_kernel(zero_ref, q_ref, k_ref, v_ref, bias_ref, o_ref, g4q, g4k, g4v, qs, ks, s1, s4, s16, p_buf, *rest):
    n_br = len(DILATIONS)
    v_ones = rest[0:2 * n_br]
    mx = rest[2 * n_br:4 * n_br]
    acc = rest[4 * n_br:6 * n_br]
    m_nat = rest[6 * n_br:6 * n_br + 2]
    m_d4 = rest[6 * n_br + 2:6 * n_br + 4]
    s_bufs = (s1, s4, s16)
    seq = q_ref.shape[0]
    n_blocks = seq // BLK
    quarter = seq // 4
    low = lax.broadcasted_iota(jnp.int32, (1, LANES), 1) < D_HEAD
    nt = (((1,), (1,)), ((), ()))
    head_lanes = (low, jnp.logical_not(low))
    z = zero_ref[0]

    def later(sl):
        return pl.ds(sl.start + z, sl.size, stride=sl.stride)

    def fill(br, dst, qv, kv, vv):
        qs[dst, :] = qv.astype(BF16)
        ks[dst, :] = kv.astype(BF16)
        v_ones[2 * br][dst, :] = jnp.where(low, vv, 1.0).astype(BF16)
        v_ones[2 * br + 1][dst, :] = jnp.where(low, 1.0, vv).astype(BF16)

    def block_list(d):
        per_class = seq // d // BLK
        blocks = []
        for b in range(n_blocks):
            r, j = divmod(b, per_class)
            rows = pl.ds(b * BLK, BLK)
            keys = rows if j == 0 else pl.ds((b - 1) * BLK, 2 * BLK)
            bcols = pl.ds(BLK, BLK) if j == 0 else pl.ds(0, 2 * BLK)
            scols = pl.ds(0, BLK) if d == 16 else bcols
            out = pl.ds((r % 4) * quarter + r // 4, BLK, stride=4) if d == 16 else rows
            blocks.append((b, rows, keys, bcols, scols, out))
        return blocks

    for br, d in enumerate(DILATIONS):
        if d == 1:
            for c in range(4):
                rows = pl.ds(c * quarter, quarter)
                fill(br, rows, q_ref[rows, :], k_ref[rows, :], v_ref[rows, :])
        elif d == 4:
            for r in range(4):
                src, dst = pl.ds(r, quarter, stride=4), pl.ds(r * quarter, quarter)
                qv, kv, vv = q_ref[src, :], k_ref[src, :], v_ref[src, :]
                g4q[dst, :], g4k[dst, :], g4v[dst, :] = qv, kv, vv
                fill(br, dst, qv, kv, vv)
        else:
            for r in range(16):
                src, dst = pl.ds((r % 4) * quarter + r // 4, BLK, stride=4), pl.ds(r * BLK, BLK)
                fill(br, dst, g4q[src, :], g4k[src, :], g4v[src, :])
        for b, rows, keys, bcols, scols, out in block_list(d):
            for hh in range(2):
                q_head = jnp.where(head_lanes[hh], qs[rows, :], jnp.zeros((), BF16))
                s = (lax.dot_general(q_head, ks[keys, :], nt, preferred_element_type=F32)
                     + bias_ref[br, hh, :, bcols])
                s_bufs[br][hh, b, :, scols] = s
                mx[2 * br + hh][out, :] = jnp.broadcast_to(jnp.max(s, axis=-1, keepdims=True), (BLK, LANES))

    for t in range(n_blocks):
        r4, n = divmod(t, 4)
        rows4 = pl.ds(t * BLK, BLK)
        nat = pl.ds(4 * BLK * n + r4, BLK, stride=4)
        for hh in range(2):
            m = jnp.maximum(jnp.maximum(mx[hh][later(nat), :], mx[2 + hh][later(rows4), :]),
                            mx[4 + hh][later(rows4), :])
            m_d4[hh][rows4, :] = m
            m_nat[hh][nat, :] = m

    for br, d in enumerate(DILATIONS):
        blocks = block_list(d)
        for b, rows, keys, bcols, scols, out in blocks:
            for hh in range(2):
                s = s_bufs[br][hh, b + z, :, scols]
                m = m_nat[hh][later(rows), :] if d == 1 else m_d4[hh][later(out), :]
                if s.shape[1] == 2 * BLK:
                    m = jnp.concatenate([m, m], axis=-1)
                p_buf[hh, b, :, scols] = jnp.exp2(s - m).astype(BF16)
        for b, rows, keys, bcols, scols, out in blocks:
            for hh in range(2):
                acc[2 * br + hh][out, :] = _dot(p_buf[hh, b, :, scols], v_ones[2 * br + hh][keys, :])

    for t in range(n_blocks):
        r4, n = divmod(t, 4)
        rows4 = pl.ds(t * BLK, BLK)
        nat = pl.ds(4 * BLK * n + r4, BLK, stride=4)
        y = []
        for hh in range(2):
            tot = acc[hh][later(nat), :] + acc[2 + hh][later(rows4), :] + acc[4 + hh][later(rows4), :]
            y.append(tot / pltpu.roll(tot, D_HEAD, axis=1))
        o_ref[nat, :] = jnp.where(low, y[0], y[1])


def _attention(qkv, bias_tab, batch, seq):
    qkv3 = qkv.reshape(batch, seq, 3 * B_WIDTH)
    pairs = B_WIDTH // LANES
    n_br = len(DILATIONS)
    blk = lambda off: pl.BlockSpec((None, seq, LANES), lambda g, b: (b, 0, off * pairs + g))
    return pl.pallas_call(
        _attn_kernel,
        grid=(pairs, batch),
        in_specs=[pl.BlockSpec(memory_space=pltpu.SMEM), blk(0), blk(1), blk(2),
                  pl.BlockSpec((n_br, 2, BLK, 2 * BLK), lambda g, b: (0, g, 0, 0))],
        out_specs=pl.BlockSpec((None, seq, LANES), lambda g, b: (b, 0, g)),
        out_shape=jax.ShapeDtypeStruct((batch, seq, B_WIDTH), F32),
        scratch_shapes=([pltpu.VMEM((seq, LANES), F32)] * 3 + [pltpu.VMEM((seq, LANES), BF16)] * 2
                        + [pltpu.VMEM((2, seq // BLK, BLK, 2 * BLK), F32)] * 2
                        + [pltpu.VMEM((2, seq // BLK, BLK, BLK), F32),
                           pltpu.VMEM((2, seq // BLK, BLK, 2 * BLK), BF16)]
                        + [pltpu.VMEM((seq, LANES), BF16)] * (2 * n_br)
                        + [pltpu.VMEM((seq, LANES), F32)] * (4 * n_br + 4)),
        compiler_params=pltpu.CompilerParams(dimension_semantics=("arbitrary", "arbitrary"),
                                             vmem_limit_bytes=VMEM_LIMIT),
        name="dilated_attn",
    )(jnp.zeros((1,), jnp.int32), qkv3, qkv3, qkv3, bias_tab)


def _ffn_kernel(x_ref, ya_ref, yb_ref, yc_ref, gb_ref, wout_ref, gf_ref, wup_ref, cfw_ref, wdn_ref, fg_ref,
                o_ref, ubuf, act, *, tiles_per_seq, final):
    t_rows = x_ref.shape[0]

    @pl.when(pl.program_id(0) % tiles_per_seq == 0)
    def _():
        ubuf[0:A_CARRY, :] = jnp.zeros((A_CARRY, 2 * D_FF), F32)

    y = jnp.concatenate([ya_ref[...], _rms(yb_ref[...], gb_ref[...]).astype(BF16), yc_ref[...]], axis=-1)
    x1 = x_ref[...] + _dot(y, wout_ref[...])
    h = _rms(x1, gf_ref[...]).astype(BF16)

    def conv3(c0):
        cols = pl.ds(c0, FF_CHUNK)
        up = _dot(h, wup_ref[:, cols])
        ubuf[A_CARRY:A_CARRY + t_rows, cols] = up
        w = cfw_ref[:, cols]
        out = (w[0:1] * ubuf[A_CARRY - 2:A_CARRY - 2 + t_rows, cols]
               + w[1:2] * ubuf[A_CARRY - 1:A_CARRY - 1 + t_rows, cols]
               + w[2:3] * up)
        ubuf[0:A_CARRY, cols] = ubuf[t_rows:t_rows + A_CARRY, cols]
        return out

    for c in range(D_FF // FF_CHUNK):
        gate = conv3(c * FF_CHUNK)
        val = conv3(D_FF + c * FF_CHUNK)
        act[:, c * FF_CHUNK:(c + 1) * FF_CHUNK] = (gate * jax.nn.sigmoid(gate) * val).astype(BF16)

    x2 = x1 + _dot(act[...], wdn_ref[...])
    if final:
        x2 = _rms(x2, fg_ref[...])
    o_ref[...] = x2


def _ffn(x, ya, yb, yc, layer, seq, final, gb, w_out, gf, w_up, cfw, w_dn, fg):
    n = x.shape[0]
    tile = lambda width: pl.BlockSpec((T_FFN, width), lambda i: (i, 0))
    row = lambda width: pl.BlockSpec((None, 1, width), lambda i: (layer, 0, 0))
    weight = lambda r, c: pl.BlockSpec((None, r, c), lambda i: (layer, 0, 0), pipeline_mode=pl.Buffered(1))
    return pl.pallas_call(
        functools.partial(_ffn_kernel, tiles_per_seq=seq // T_FFN, final=final),
        grid=(n // T_FFN,),
        in_specs=[tile(D_MODEL), tile(A_WIDTH), tile(B_WIDTH), tile(C_WIDTH),
                  row(B_WIDTH), weight(D_MODEL, D_MODEL), row(D_MODEL), weight(D_MODEL, 2 * D_FF),
                  pl.BlockSpec((None, FFN_CONV, 2 * D_FF), lambda i: (layer, 0, 0)),
                  weight(D_FF, D_MODEL),
                  pl.BlockSpec((1, D_MODEL), lambda i: (0, 0))],
        out_specs=tile(D_MODEL),
        out_shape=jax.ShapeDtypeStruct((n, D_MODEL), F32),
        scratch_shapes=[pltpu.VMEM((A_CARRY + T_FFN, 2 * D_FF), F32),
                        pltpu.VMEM((T_FFN, D_FF), BF16)],
        compiler_params=pltpu.CompilerParams(dimension_semantics=("arbitrary",), vmem_limit_bytes=VMEM_LIMIT),
        name="outproj_ffn",
    )(x, ya, yb, yc, gb, w_out, gf, w_up, cfw, w_dn, fg)


def kernel(x, norm_mix_g, w_in, conv_a_w, conv_c_w, conv_c_b, ln_c_g, ln_c_b, out_norm_g, w_out, norm_ffn_g, w_up, conv_f_w, w_down, rel_bias, final_g):
    batch, seq, d_model = x.shape
    depth = w_in.shape[0]
    assert d_model == D_MODEL and seq % T_IN == 0 and seq % T_FFN == 0 and seq % (BLK * max(DILATIONS)) == 0
    assert w_in.shape[2] == IN_COLS and w_up.shape[2] == 2 * D_FF and D_FF % FF_CHUNK == 0

    rows = lambda a: a.reshape(depth, 1, a.shape[-1])
    w_in_b, w_out_b, w_up_b, w_dn_b = (w.astype(BF16) for w in (w_in, w_out, w_up, w_down))
    g_mix, g_ffn, ccb, lng, lnb = (rows(a) for a in (norm_mix_g, norm_ffn_g, conv_c_b, ln_c_g, ln_c_b))
    g_a = rows(out_norm_g[:, 0:A_WIDTH])
    g_b = rows(out_norm_g[:, A_WIDTH:A_WIDTH + B_WIDTH])
    g_c = rows(out_norm_g[:, A_WIDTH + B_WIDTH:])
    fg = final_g.reshape(1, D_MODEL)

    bias_tab = _bias_table(rel_bias)
    xf = x.reshape(batch * seq, D_MODEL)
    for layer in range(depth):
        ya, qkv, yc = _inproj(xf, layer, seq, g_mix, w_in_b, conv_a_w, conv_c_w, ccb, lng, lnb, g_a, g_c)
        yb = _attention(qkv, bias_tab, batch, seq).reshape(batch * seq, B_WIDTH)
        xf = _ffn(xf, ya, yb, yc, layer, seq, layer == depth - 1, g_b, w_out_b, g_ffn, w_up_b, conv_f_w, w_dn_b, fg)
    return xf.reshape(batch, seq, D_MODEL)
```

```python
import functools
import math

import numpy as np
import jax
import jax.numpy as jnp
from jax import lax
from jax.experimental import pallas as pl
from jax.experimental.pallas import tpu as pltpu

F32 = jnp.float32
BF16 = jnp.bfloat16

D_MODEL = 1024
D_HEAD = 64
A_WIDTH = 256
B_HEADS = 8
B_WIDTH = 512
C_WIDTH = 256
IN_COLS = 3 * A_WIDTH + 3 * B_WIDTH + 2 * C_WIDTH
QKV_LO = 3 * A_WIDTH
C_LO = QKV_LO + 3 * B_WIDTH
DILATIONS = (1, 4, 16)
N_KEYS = 128
BLK = 128
NUM_BUCKETS = 32
MAX_DISTANCE = 2048
SHORT_CONV = 3
CONFORMER_CONV = 31
FFN_CONV = 3
D_FF = 2816
EPS = 1e-6
NEG = -1e30
LOG2E = math.log2(math.e)

LANES = 128
SUBLANES = 8
VMEM_LIMIT = 56 * 1024 * 1024

T_IN = 512
T_FFN = 512
CONV_ROWS = 64
FF_CHUNK = 256
A_CARRY = SUBLANES
C_CARRY = 32


def _rms(x, g):
    return x * lax.rsqrt(jnp.mean(x * x, axis=-1, keepdims=True) + EPS) * g


def _dot(a, b):
    return jnp.dot(a, b, preferred_element_type=F32)


def _bucket_table():
    rel = np.arange(BLK)[:, None] - np.arange(2 * BLK)[None, :] + BLK
    valid = (rel >= 0) & (rel <= N_KEYS)
    max_exact = NUM_BUCKETS // 2
    out = []
    for d in DILATIONS:
        dist = np.maximum(rel, 0) * d
        d_f = np.maximum(dist, 1).astype(np.float32)
        large = max_exact + (np.log(d_f / np.float32(max_exact)) / np.float32(math.log(MAX_DISTANCE / max_exact))
                             * np.float32(NUM_BUCKETS - max_exact)).astype(np.int32)
        large = np.minimum(large, NUM_BUCKETS - 1)
        bucket = np.where(dist < max_exact, dist, large)
        out.append(np.where(valid, bucket, -1).astype(np.int32))
    return np.stack(out)


def _bias_kernel(rb_ref, bucket_ref, o_ref):
    h = pl.program_id(1)
    bucket = bucket_ref[...]
    acc = jnp.full(bucket.shape, NEG, F32)
    for b in range(NUM_BUCKETS):
        acc = jnp.where(bucket == b, rb_ref[b, h] * LOG2E, acc)
    o_ref[...] = acc


def _bias_table(rel_bias):
    buckets = jnp.asarray(_bucket_table())
    return pl.pallas_call(
        _bias_kernel,
        grid=(len(DILATIONS), B_HEADS),
        in_specs=[pl.BlockSpec(memory_space=pltpu.SMEM),
                  pl.BlockSpec((None, BLK, 2 * BLK), lambda br, h: (br, 0, 0))],
        out_specs=pl.BlockSpec((None, None, BLK, 2 * BLK), lambda br, h: (br, h, 0, 0)),
        out_shape=jax.ShapeDtypeStruct((len(DILATIONS), B_HEADS, BLK, 2 * BLK), F32),
        name="bias_table",
    )(rel_bias, buckets)


def _inproj_kernel(x_ref, g_ref, w_ref, caw_ref, ccw_ref, ccb_ref, lng_ref, lnb_ref, ga_ref, gc_ref,
                   ya_ref, qkv_ref, yc_ref, tbuf, ubuf, ushift, *, tiles_per_seq):
    t_rows = x_ref.shape[0]

    @pl.when(pl.program_id(0) % tiles_per_seq == 0)
    def _():
        tbuf[0:A_CARRY, :] = jnp.zeros((A_CARRY, A_WIDTH), F32)
        ubuf[0:C_CARRY, :] = jnp.zeros((C_CARRY, C_WIDTH), F32)

    h = _rms(x_ref[...], g_ref[...]).astype(BF16)

    zc = _dot(h, w_ref[:, C_LO:IN_COLS])
    ubuf[C_CARRY:C_CARRY + t_rows, :] = zc[:, 0:C_WIDTH] * jax.nn.sigmoid(zc[:, C_WIDTH:2 * C_WIDTH])
    ccw = ccw_ref[...]
    first = C_CARRY - (CONFORMER_CONV - 1)
    shift_rows = C_CARRY + t_rows - SUBLANES
    for ph in range(1, SUBLANES):
        ushift[ph - 1, :, :] = ubuf[pl.ds(ph, shift_rows), :]
    for rc in range(t_rows // CONV_ROWS):
        acc = jnp.broadcast_to(ccb_ref[...], (CONV_ROWS, C_WIDTH))
        for k in range(CONFORMER_CONV):
            tiles, ph = divmod(first + k, SUBLANES)
            rows = pl.ds(rc * CONV_ROWS + tiles * SUBLANES, CONV_ROWS)
            acc = acc + ccw[k:k + 1] * (ubuf[rows, :] if ph == 0 else ushift[ph - 1, rows, :])
        mu = jnp.mean(acc, axis=-1, keepdims=True)
        cen = acc - mu
        var = jnp.mean(cen * cen, axis=-1, keepdims=True)
        ln = cen * lax.rsqrt(var + EPS) * lng_ref[...] + lnb_ref[...]
        yc = ln * jax.nn.sigmoid(ln)
        yc_ref[rc * CONV_ROWS:(rc + 1) * CONV_ROWS, :] = _rms(yc, gc_ref[...]).astype(BF16)
    ubuf[0:C_CARRY, :] = ubuf[t_rows:t_rows + C_CARRY, :]

    za = _dot(h, w_ref[:, 0:QKV_LO])
    a_h, a_b, a_c = za[:, 0:A_WIDTH], za[:, A_WIDTH:2 * A_WIDTH], za[:, 2 * A_WIDTH:3 * A_WIDTH]
    t = a_c * a_h
    tbuf[A_CARRY:A_CARRY + t_rows, :] = t
    caw = caw_ref[...]
    conv = (caw[0:1] * tbuf[A_CARRY - 2:A_CARRY - 2 + t_rows, :]
            + caw[1:2] * tbuf[A_CARRY - 1:A_CARRY - 1 + t_rows, :]
            + caw[2:3] * t)
    tbuf[0:A_CARRY, :] = tbuf[t_rows:t_rows + A_CARRY, :]
    ya_ref[...] = _rms(a_b * conv, ga_ref[...]).astype(BF16)

    zqkv = _dot(h, w_ref[:, QKV_LO:C_LO])
    pairs = B_WIDTH // LANES
    for g in range(3 * pairs):
        slab = zqkv[:, g * LANES:(g + 1) * LANES]
        qkv_ref[g, :, :] = slab * (D_HEAD ** -0.5 * LOG2E) if g < pairs else slab


def _inproj(x, layer, seq, g, w_in, caw, ccw, ccb, lng, lnb, ga, gc):
    n = x.shape[0]
    tiles_per_seq = seq // T_IN
    slabs = 3 * B_WIDTH // LANES
    row = lambda width: pl.BlockSpec((None, 1, width), lambda i: (layer, 0, 0))
    return pl.pallas_call(
        functools.partial(_inproj_kernel, tiles_per_seq=tiles_per_seq),
        grid=(n // T_IN,),
        in_specs=[pl.BlockSpec((T_IN, D_MODEL), lambda i: (i, 0)),
                  row(D_MODEL),
                  pl.BlockSpec((None, D_MODEL, IN_COLS), lambda i: (layer, 0, 0), pipeline_mode=pl.Buffered(1)),
                  pl.BlockSpec((None, SHORT_CONV, A_WIDTH), lambda i: (layer, 0, 0)),
                  pl.BlockSpec((None, CONFORMER_CONV, C_WIDTH), lambda i: (layer, 0, 0)),
                  row(C_WIDTH), row(C_WIDTH), row(C_WIDTH), row(A_WIDTH), row(C_WIDTH)],
        out_specs=[pl.BlockSpec((T_IN, A_WIDTH), lambda i: (i, 0)),
                   pl.BlockSpec((None, slabs, T_IN, LANES),
                                lambda i: (i // tiles_per_seq, 0, i % tiles_per_seq, 0)),
                   pl.BlockSpec((T_IN, C_WIDTH), lambda i: (i, 0))],
        out_shape=[jax.ShapeDtypeStruct((n, A_WIDTH), BF16),
                   jax.ShapeDtypeStruct((n // seq, slabs, seq, LANES), F32),
                   jax.ShapeDtypeStruct((n, C_WIDTH), BF16)],
        scratch_shapes=[pltpu.VMEM((A_CARRY + T_IN, A_WIDTH), F32),
                        pltpu.VMEM((C_CARRY + T_IN, C_WIDTH), F32),
                        pltpu.VMEM((SUBLANES - 1, C_CARRY + T_IN - SUBLANES, C_WIDTH), F32)],
        compiler_params=pltpu.CompilerParams(dimension_semantics=("arbitrary",), vmem_limit_bytes=VMEM_LIMIT),
        name="inproj_mix",
    )(x, g, w_in, caw, ccw, ccb, lng, lnb, ga, gc)


def _attn_kernel(q_ref, k_ref, v_ref, bias_ref, o_ref, g4q, g4k, g4v, qs, ks, v0s, v1s, s_buf, p_buf, *stats):
    seq = q_ref.shape[0]
    n_blocks = seq // BLK
    quarter = seq // 4
    low = lax.broadcasted_iota(jnp.int32, (1, LANES), 1) < D_HEAD
    nt = (((1,), (1,)), ((), ()))
    head_lanes = (low, jnp.logical_not(low))
    v_ones = (v0s, v1s)

    def fill(dst, qv, kv, vv):
        qs[dst, :] = qv.astype(BF16)
        ks[dst, :] = kv.astype(BF16)
        v0s[dst, :] = jnp.where(low, vv, 1.0).astype(BF16)
        v1s[dst, :] = jnp.where(low, 1.0, vv).astype(BF16)

    for br, d in enumerate(DILATIONS):
        if d == 1:
            for c in range(4):
                rows = pl.ds(c * quarter, quarter)
                fill(rows, q_ref[rows, :], k_ref[rows, :], v_ref[rows, :])
        elif d == 4:
            for r in range(4):
                src, dst = pl.ds(r, quarter, stride=4), pl.ds(r * quarter, quarter)
                qv, kv, vv = q_ref[src, :], k_ref[src, :], v_ref[src, :]
                g4q[dst, :], g4k[dst, :], g4v[dst, :] = qv, kv, vv
                fill(dst, qv, kv, vv)
        else:
            for r in range(16):
                src, dst = pl.ds((r % 4) * quarter + r // 4, BLK, stride=4), pl.ds(r * BLK, BLK)
                fill(dst, g4q[src, :], g4k[src, :], g4v[src, :])

        per_class = seq // d // BLK
        blocks = []
        for b in range(n_blocks):
            r, j = divmod(b, per_class)
            rows = pl.ds(b * BLK, BLK)
            keys = rows if j == 0 else pl.ds((b - 1) * BLK, 2 * BLK)
            cols = pl.ds(BLK, BLK) if j == 0 else pl.ds(0, 2 * BLK)
            out = pl.ds((r % 4) * quarter + r // 4, BLK, stride=4) if d == 16 else rows
            blocks.append((b, rows, keys, cols, out))

        for b, rows, keys, cols, out in blocks:
            for hh in range(2):
                q_head = jnp.where(head_lanes[hh], qs[rows, :], jnp.zeros((), BF16))
                s_buf[hh, b, :, cols] = (lax.dot_general(q_head, ks[keys, :], nt, preferred_element_type=F32)
                                         + bias_ref[br, hh, :, cols])
        for b, rows, keys, cols, out in blocks:
            for hh in range(2):
                s = s_buf[hh, b, :, cols]
                m = jnp.max(s, axis=-1, keepdims=True)
                p_buf[hh, b, :, cols] = jnp.exp2(s - m).astype(BF16)
                stats[2 * (2 * br + hh) + 1][out, :] = jnp.broadcast_to(m, (BLK, LANES))
        for b, rows, keys, cols, out in blocks:
            for hh in range(2):
                stats[2 * (2 * br + hh)][out, :] = _dot(p_buf[hh, b, :, cols], v_ones[hh][keys, :])

    for t in range(n_blocks):
        r4, n = divmod(t, 4)
        rows4 = pl.ds(t * BLK, BLK)
        nat = pl.ds(4 * BLK * n + r4, BLK, stride=4)
        y = []
        for hh in range(2):
            where = (nat, rows4, rows4)
            ms = [stats[2 * (2 * br + hh) + 1][where[br], :] for br in range(len(DILATIONS))]
            m_all = jnp.maximum(jnp.maximum(ms[0], ms[1]), ms[2])
            tot = jnp.zeros((BLK, LANES), F32)
            for br in range(len(DILATIONS)):
                tot = tot + jnp.exp2(ms[br] - m_all) * stats[2 * (2 * br + hh)][where[br], :]
            y.append(tot / pltpu.roll(tot, D_HEAD, axis=1))
        o_ref[nat, :] = jnp.where(low, y[0], y[1])


def _attention(qkv, bias_tab):
    batch, _, seq, _ = qkv.shape
    pairs = B_WIDTH // LANES
    n_br = len(DILATIONS)
    blk = lambda off: pl.BlockSpec((None, None, seq, LANES), lambda g, b: (b, off * pairs + g, 0, 0))
    return pl.pallas_call(
        _attn_kernel,
        grid=(pairs, batch),
        in_specs=[blk(0), blk(1), blk(2),
                  pl.BlockSpec((n_br, 2, BLK, 2 * BLK), lambda g, b: (0, g, 0, 0))],
        out_specs=pl.BlockSpec((None, seq, LANES), lambda g, b: (b, 0, g)),
        out_shape=jax.ShapeDtypeStruct((batch, seq, B_WIDTH), F32),
        scratch_shapes=([pltpu.VMEM((seq, LANES), F32)] * 3 + [pltpu.VMEM((seq, LANES), BF16)] * 4
                        + [pltpu.VMEM((2, seq // BLK, BLK, 2 * BLK), F32),
                           pltpu.VMEM((2, seq // BLK, BLK, 2 * BLK), BF16)]
                        + [pltpu.VMEM((seq, LANES), F32)] * (4 * n_br)),
        compiler_params=pltpu.CompilerParams(dimension_semantics=("arbitrary", "arbitrary"),
                                             vmem_limit_bytes=VMEM_LIMIT),
        name="dilated_attn",
    )(qkv, qkv, qkv, bias_tab)


def _ffn_kernel(x_ref, ya_ref, yb_ref, yc_ref, gb_ref, wout_ref, gf_ref, wup_ref, cfw_ref, wdn_ref, fg_ref,
                o_ref, ubuf, act, *, tiles_per_seq, final):
    t_rows = x_ref.shape[0]

    @pl.when(pl.program_id(0) % tiles_per_seq == 0)
    def _():
        ubuf[0:A_CARRY, :] = jnp.zeros((A_CARRY, 2 * D_FF), F32)

    y = jnp.concatenate([ya_ref[...], _rms(yb_ref[...], gb_ref[...]).astype(BF16), yc_ref[...]], axis=-1)
    x1 = x_ref[...] + _dot(y, wout_ref[...])
    h = _rms(x1, gf_ref[...]).astype(BF16)

    def conv3(c0):
        cols = pl.ds(c0, FF_CHUNK)
        up = _dot(h, wup_ref[:, cols])
        ubuf[A_CARRY:A_CARRY + t_rows, cols] = up
        w = cfw_ref[:, cols]
        out = (w[0:1] * ubuf[A_CARRY - 2:A_CARRY - 2 + t_rows, cols]
               + w[1:2] * ubuf[A_CARRY - 1:A_CARRY - 1 + t_rows, cols]
               + w[2:3] * up)
        ubuf[0:A_CARRY, cols] = ubuf[t_rows:t_rows + A_CARRY, cols]
        return out

    for c in range(D_FF // FF_CHUNK):
        gate = conv3(c * FF_CHUNK)
        val = conv3(D_FF + c * FF_CHUNK)
        act[:, c * FF_CHUNK:(c + 1) * FF_CHUNK] = (gate * jax.nn.sigmoid(gate) * val).astype(BF16)

    x2 = x1 + _dot(act[...], wdn_ref[...])
    if final:
        x2 = _rms(x2, fg_ref[...])
    o_ref[...] = x2


def _ffn(x, ya, yb, yc, layer, seq, final, gb, w_out, gf, w_up, cfw, w_dn, fg):
    n = x.shape[0]
    tile = lambda width: pl.BlockSpec((T_FFN, width), lambda i: (i, 0))
    row = lambda width: pl.BlockSpec((None, 1, width), lambda i: (layer, 0, 0))
    weight = lambda r, c: pl.BlockSpec((None, r, c), lambda i: (layer, 0, 0), pipeline_mode=pl.Buffered(1))
    return pl.pallas_call(
        functools.partial(_ffn_kernel, tiles_per_seq=seq // T_FFN, final=final),
        grid=(n // T_FFN,),
        in_specs=[tile(D_MODEL), tile(A_WIDTH), tile(B_WIDTH), tile(C_WIDTH),
                  row(B_WIDTH), weight(D_MODEL, D_MODEL), row(D_MODEL), weight(D_MODEL, 2 * D_FF),
                  pl.BlockSpec((None, FFN_CONV, 2 * D_FF), lambda i: (layer, 0, 0)),
                  weight(D_FF, D_MODEL),
                  pl.BlockSpec((1, D_MODEL), lambda i: (0, 0))],
        out_specs=tile(D_MODEL),
        out_shape=jax.ShapeDtypeStruct((n, D_MODEL), F32),
        scratch_shapes=[pltpu.VMEM((A_CARRY + T_FFN, 2 * D_FF), F32),
                        pltpu.VMEM((T_FFN, D_FF), BF16)],
        compiler_params=pltpu.CompilerParams(dimension_semantics=("arbitrary",), vmem_limit_bytes=VMEM_LIMIT),
        name="outproj_ffn",
    )(x, ya, yb, yc, gb, w_out, gf, w_up, cfw, w_dn, fg)


def kernel(x, norm_mix_g, w_in, conv_a_w, conv_c_w, conv_c_b, ln_c_g, ln_c_b, out_norm_g, w_out, norm_ffn_g, w_up, conv_f_w, w_down, rel_bias, final_g):
    batch, seq, d_model = x.shape
    depth = w_in.shape[0]
    assert d_model == D_MODEL and seq % T_IN == 0 and seq % T_FFN == 0 and seq % (BLK * max(DILATIONS)) == 0
    assert w_in.shape[2] == IN_COLS and w_up.shape[2] == 2 * D_FF and D_FF % FF_CHUNK == 0

    rows = lambda a: a.reshape(depth, 1, a.shape[-1])
    w_in_b, w_out_b, w_up_b, w_dn_b = (w.astype(BF16) for w in (w_in, w_out, w_up, w_down))
    g_mix, g_ffn, ccb, lng, lnb = (rows(a) for a in (norm_mix_g, norm_ffn_g, conv_c_b, ln_c_g, ln_c_b))
    g_a = rows(out_norm_g[:, 0:A_WIDTH])
    g_b = rows(out_norm_g[:, A_WIDTH:A_WIDTH + B_WIDTH])
    g_c = rows(out_norm_g[:, A_WIDTH + B_WIDTH:])
    fg = final_g.reshape(1, D_MODEL)

    bias_tab = _bias_table(rel_bias)
    xf = x.reshape(batch * seq, D_MODEL)
    for layer in range(depth):
        ya, qkv, yc = _inproj(xf, layer, seq, g_mix, w_in_b, conv_a_w, conv_c_w, ccb, lng, lnb, g_a, g_c)
        yb = _attention(qkv, bias_tab).reshape(batch * seq, B_WIDTH)
        xf = _ffn(xf, ya, yb, yc, layer, seq, layer == depth - 1, g_b, w_out_b, g_ffn, w_up_b, conv_f_w, w_dn_b, fg)
    return xf.reshape(batch, seq, D_MODEL)
```

```python
import functools
import math

import numpy as np
import jax
import jax.numpy as jnp
from jax import lax
from jax.experimental import pallas as pl
from jax.experimental.pallas import tpu as pltpu

F32 = jnp.float32
BF16 = jnp.bfloat16

D_MODEL = 1024
D_HEAD = 64
A_WIDTH = 256
B_HEADS = 8
B_WIDTH = 512
C_WIDTH = 256
IN_COLS = 3 * A_WIDTH + 3 * B_WIDTH + 2 * C_WIDTH
QKV_LO = 3 * A_WIDTH
C_LO = QKV_LO + 3 * B_WIDTH
DILATIONS = (1, 4, 16)
N_KEYS = 128
BLK = 128
NUM_BUCKETS = 32
MAX_DISTANCE = 2048
SHORT_CONV = 3
CONFORMER_CONV = 31
FFN_CONV = 3
D_FF = 2816
EPS = 1e-6
NEG = -1e30
LOG2E = math.log2(math.e)

LANES = 128
SUBLANES = 8
VMEM_LIMIT = 56 * 1024 * 1024

T_IN = 512
T_FFN = 512
T_FUSED = 256
CONV_ROWS = {T_IN: 64, T_FUSED: 32}
FF_CHUNK = 256
A_CARRY = SUBLANES
C_CARRY = 32


def _rms(x, g):
    return x * lax.rsqrt(jnp.mean(x * x, axis=-1, keepdims=True) + EPS) * g


def _dot(a, b):
    return jnp.dot(a, b, preferred_element_type=F32)


def _bucket_table():
    rel = np.arange(BLK)[:, None] - np.arange(2 * BLK)[None, :] + BLK
    valid = (rel >= 0) & (rel <= N_KEYS)
    max_exact = NUM_BUCKETS // 2
    out = []
    for d in DILATIONS:
        dist = np.maximum(rel, 0) * d
        d_f = np.maximum(dist, 1).astype(np.float32)
        large = max_exact + (np.log(d_f / np.float32(max_exact)) / np.float32(math.log(MAX_DISTANCE / max_exact))
                             * np.float32(NUM_BUCKETS - max_exact)).astype(np.int32)
        large = np.minimum(large, NUM_BUCKETS - 1)
        bucket = np.where(dist < max_exact, dist, large)
        out.append(np.where(valid, bucket, -1).astype(np.int32))
    return np.stack(out)


def _bias_kernel(rb_ref, bucket_ref, o_ref):
    h = pl.program_id(1)
    bucket = bucket_ref[...]
    acc = jnp.full(bucket.shape, NEG, F32)
    for b in range(NUM_BUCKETS):
        acc = jnp.where(bucket == b, rb_ref[b, h] * LOG2E, acc)
    o_ref[...] = acc


def _bias_table(rel_bias):
    buckets = jnp.asarray(_bucket_table())
    return pl.pallas_call(
        _bias_kernel,
        grid=(len(DILATIONS), B_HEADS),
        in_specs=[pl.BlockSpec(memory_space=pltpu.SMEM),
                  pl.BlockSpec((None, BLK, 2 * BLK), lambda br, h: (br, 0, 0))],
        out_specs=pl.BlockSpec((None, None, BLK, 2 * BLK), lambda br, h: (br, h, 0, 0)),
        out_shape=jax.ShapeDtypeStruct((len(DILATIONS), B_HEADS, BLK, 2 * BLK), F32),
        name="bias_table",
    )(rel_bias, buckets)


def _inproj_body(x, sequence_start, g_ref, w_ref, caw_ref, ccw_ref, ccb_ref, lng_ref, lnb_ref, ga_ref, gc_ref,
                 ya_ref, qkv_ref, yc_ref, tbuf, ubuf, ushift):
    t_rows = x.shape[0]
    conv_rows = CONV_ROWS[t_rows]

    @pl.when(sequence_start)
    def _():
        tbuf[0:A_CARRY, :] = jnp.zeros((A_CARRY, A_WIDTH), F32)
        ubuf[0:C_CARRY, :] = jnp.zeros((C_CARRY, C_WIDTH), F32)

    h = _rms(x, g_ref[...]).astype(BF16)

    zc = _dot(h, w_ref[:, C_LO:IN_COLS])
    ubuf[C_CARRY:C_CARRY + t_rows, :] = zc[:, 0:C_WIDTH] * jax.nn.sigmoid(zc[:, C_WIDTH:2 * C_WIDTH])
    ccw = ccw_ref[...]
    first = C_CARRY - (CONFORMER_CONV - 1)
    shift_rows = C_CARRY + t_rows - SUBLANES
    for ph in range(1, SUBLANES):
        ushift[ph - 1, :, :] = ubuf[pl.ds(ph, shift_rows), :]
    for rc in range(t_rows // conv_rows):
        acc = jnp.broadcast_to(ccb_ref[...], (conv_rows, C_WIDTH))
        for k in range(CONFORMER_CONV):
            tiles, ph = divmod(first + k, SUBLANES)
            rows = pl.ds(rc * conv_rows + tiles * SUBLANES, conv_rows)
            acc = acc + ccw[k:k + 1] * (ubuf[rows, :] if ph == 0 else ushift[ph - 1, rows, :])
        mu = jnp.mean(acc, axis=-1, keepdims=True)
        cen = acc - mu
        var = jnp.mean(cen * cen, axis=-1, keepdims=True)
        ln = cen * lax.rsqrt(var + EPS) * lng_ref[...] + lnb_ref[...]
        yc = ln * jax.nn.sigmoid(ln)
        yc_ref[rc * conv_rows:(rc + 1) * conv_rows, :] = _rms(yc, gc_ref[...]).astype(BF16)
    ubuf[0:C_CARRY, :] = ubuf[t_rows:t_rows + C_CARRY, :]

    za = _dot(h, w_ref[:, 0:QKV_LO])
    a_h, a_b, a_c = za[:, 0:A_WIDTH], za[:, A_WIDTH:2 * A_WIDTH], za[:, 2 * A_WIDTH:3 * A_WIDTH]
    t = a_c * a_h
    tbuf[A_CARRY:A_CARRY + t_rows, :] = t
    caw = caw_ref[...]
    conv = (caw[0:1] * tbuf[A_CARRY - 2:A_CARRY - 2 + t_rows, :]
            + caw[1:2] * tbuf[A_CARRY - 1:A_CARRY - 1 + t_rows, :]
            + caw[2:3] * t)
    tbuf[0:A_CARRY, :] = tbuf[t_rows:t_rows + A_CARRY, :]
    ya_ref[...] = _rms(a_b * conv, ga_ref[...]).astype(BF16)

    zqkv = _dot(h, w_ref[:, QKV_LO:C_LO])
    pairs = B_WIDTH // LANES
    for g in range(3 * pairs):
        slab = zqkv[:, g * LANES:(g + 1) * LANES]
        qkv_ref[g, :, :] = slab * (D_HEAD ** -0.5 * LOG2E) if g < pairs else slab


def _inproj_kernel(x_ref, *refs, tiles_per_seq):
    _inproj_body(x_ref[...], pl.program_id(0) % tiles_per_seq == 0, *refs)


def _inproj_in_specs(layer):
    row = lambda width: pl.BlockSpec((None, 1, width), lambda i: (layer, 0, 0))
    return [row(D_MODEL),
            pl.BlockSpec((None, D_MODEL, IN_COLS), lambda i: (layer, 0, 0), pipeline_mode=pl.Buffered(1)),
            pl.BlockSpec((None, SHORT_CONV, A_WIDTH), lambda i: (layer, 0, 0)),
            pl.BlockSpec((None, CONFORMER_CONV, C_WIDTH), lambda i: (layer, 0, 0)),
            row(C_WIDTH), row(C_WIDTH), row(C_WIDTH), row(A_WIDTH), row(C_WIDTH)]


def _inproj_outputs(n, seq, t_rows, tile_index):
    tiles_per_seq = seq // t_rows
    slabs = 3 * B_WIDTH // LANES
    specs = [pl.BlockSpec((t_rows, A_WIDTH), lambda i: (tile_index(i), 0)),
             pl.BlockSpec((None, slabs, t_rows, LANES),
                          lambda i: (tile_index(i) // tiles_per_seq, 0, tile_index(i) % tiles_per_seq, 0)),
             pl.BlockSpec((t_rows, C_WIDTH), lambda i: (tile_index(i), 0))]
    shapes = [jax.ShapeDtypeStruct((n, A_WIDTH), BF16),
              jax.ShapeDtypeStruct((n // seq, slabs, seq, LANES), F32),
              jax.ShapeDtypeStruct((n, C_WIDTH), BF16)]
    return specs, shapes


def _inproj_scratch(t_rows):
    return [pltpu.VMEM((A_CARRY + t_rows, A_WIDTH), F32),
            pltpu.VMEM((C_CARRY + t_rows, C_WIDTH), F32),
            pltpu.VMEM((SUBLANES - 1, C_CARRY + t_rows - SUBLANES, C_WIDTH), F32)]


def _inproj(x, layer, seq, *params):
    n = x.shape[0]
    out_specs, out_shape = _inproj_outputs(n, seq, T_IN, lambda i: i)
    return pl.pallas_call(
        functools.partial(_inproj_kernel, tiles_per_seq=seq // T_IN),
        grid=(n // T_IN,),
        in_specs=[pl.BlockSpec((T_IN, D_MODEL), lambda i: (i, 0))] + _inproj_in_specs(layer),
        out_specs=out_specs,
        out_shape=out_shape,
        scratch_shapes=_inproj_scratch(T_IN),
        compiler_params=pltpu.CompilerParams(dimension_semantics=("arbitrary",), vmem_limit_bytes=VMEM_LIMIT),
        name="inproj_mix",
    )(x, *params)


def _attn_kernel(q_ref, k_ref, v_ref, bias_ref, o_ref, g4q, g4k, g4v, qs, ks, v0s, v1s, *stats):
    seq = q_ref.shape[0]
    n_blocks = seq // BLK
    quarter = seq // 4
    low = lax.broadcasted_iota(jnp.int32, (1, LANES), 1) < D_HEAD
    nt = (((1,), (1,)), ((), ()))
    head_lanes = (low, jnp.logical_not(low))
    v_ones = (v0s, v1s)

    def fill(dst, qv, kv, vv):
        qs[dst, :] = qv.astype(BF16)
        ks[dst, :] = kv.astype(BF16)
        v0s[dst, :] = jnp.where(low, vv, 1.0).astype(BF16)
        v1s[dst, :] = jnp.where(low, 1.0, vv).astype(BF16)

    for br, d in enumerate(DILATIONS):
        if d == 1:
            for c in range(4):
                rows = pl.ds(c * quarter, quarter)
                fill(rows, q_ref[rows, :], k_ref[rows, :], v_ref[rows, :])
        elif d == 4:
            for r in range(4):
                src, dst = pl.ds(r, quarter, stride=4), pl.ds(r * quarter, quarter)
                qv, kv, vv = q_ref[src, :], k_ref[src, :], v_ref[src, :]
                g4q[dst, :], g4k[dst, :], g4v[dst, :] = qv, kv, vv
                fill(dst, qv, kv, vv)
        else:
            for r in range(16):
                src, dst = pl.ds((r % 4) * quarter + r // 4, BLK, stride=4), pl.ds(r * BLK, BLK)
                fill(dst, g4q[src, :], g4k[src, :], g4v[src, :])

        per_class = seq // d // BLK
        blocks = []
        for b in range(n_blocks):
            r, j = divmod(b, per_class)
            rows = pl.ds(b * BLK, BLK)
            keys = rows if j == 0 else pl.ds((b - 1) * BLK, 2 * BLK)
            cols = pl.ds(BLK, BLK) if j == 0 else pl.ds(0, 2 * BLK)
            out = pl.ds((r % 4) * quarter + r // 4, BLK, stride=4) if d == 16 else rows
            blocks.append((rows, keys, cols, out))

        scores = []
        for rows, keys, cols, out in blocks:
            for hh in range(2):
                q_head = jnp.where(head_lanes[hh], qs[rows, :], jnp.zeros((), BF16))
                scores.append(lax.dot_general(q_head, ks[keys, :], nt, preferred_element_type=F32)
                              + bias_ref[br, hh, :, cols])
        probs = []
        for i, (rows, keys, cols, out) in enumerate(blocks):
            for hh in range(2):
                s = scores[2 * i + hh]
                m = jnp.max(s, axis=-1, keepdims=True)
                probs.append(jnp.exp2(s - m).astype(BF16))
                stats[2 * (2 * br + hh) + 1][out, :] = jnp.broadcast_to(m, (BLK, LANES))
        for i, (rows, keys, cols, out) in enumerate(blocks):
            for hh in range(2):
                stats[2 * (2 * br + hh)][out, :] = _dot(probs[2 * i + hh], v_ones[hh][keys, :])

    for t in range(n_blocks):
        r4, n = divmod(t, 4)
        rows4 = pl.ds(t * BLK, BLK)
        nat = pl.ds(4 * BLK * n + r4, BLK, stride=4)
        y = []
        for hh in range(2):
            where = (nat, rows4, rows4)
            ms = [stats[2 * (2 * br + hh) + 1][where[br], :] for br in range(len(DILATIONS))]
            m_all = jnp.maximum(jnp.maximum(ms[0], ms[1]), ms[2])
            tot = jnp.zeros((BLK, LANES), F32)
            for br in range(len(DILATIONS)):
                tot = tot + jnp.exp2(ms[br] - m_all) * stats[2 * (2 * br + hh)][where[br], :]
            y.append(tot / pltpu.roll(tot, D_HEAD, axis=1))
        o_ref[nat, :] = jnp.where(low, y[0], y[1])


def _attention(qkv, bias_tab):
    batch, _, seq, _ = qkv.shape
    pairs = B_WIDTH // LANES
    n_br = len(DILATIONS)
    blk = lambda off: pl.BlockSpec((None, None, seq, LANES), lambda g, b: (b, off * pairs + g, 0, 0))
    return pl.pallas_call(
        _attn_kernel,
        grid=(pairs, batch),
        in_specs=[blk(0), blk(1), blk(2),
                  pl.BlockSpec((n_br, 2, BLK, 2 * BLK), lambda g, b: (0, g, 0, 0))],
        out_specs=pl.BlockSpec((None, seq, LANES), lambda g, b: (b, 0, g)),
        out_shape=jax.ShapeDtypeStruct((batch, seq, B_WIDTH), F32),
        scratch_shapes=([pltpu.VMEM((seq, LANES), F32)] * 3 + [pltpu.VMEM((seq, LANES), BF16)] * 4
                        + [pltpu.VMEM((seq, LANES), F32)] * (4 * n_br)),
        compiler_params=pltpu.CompilerParams(dimension_semantics=("arbitrary", "arbitrary"),
                                             vmem_limit_bytes=VMEM_LIMIT),
        name="dilated_attn",
    )(qkv, qkv, qkv, bias_tab)


def _ffn_body(x_ref, ya_ref, yb_ref, yc_ref, gb_ref, wout_ref, gf_ref, wup_ref, cfw_ref, wdn_ref, ubuf, act):
    t_rows = x_ref.shape[0]
    y = jnp.concatenate([ya_ref[...], _rms(yb_ref[...], gb_ref[...]).astype(BF16), yc_ref[...]], axis=-1)
    x1 = x_ref[...] + _dot(y, wout_ref[...])
    h = _rms(x1, gf_ref[...]).astype(BF16)

    def conv3(c0):
        cols = pl.ds(c0, FF_CHUNK)
        up = _dot(h, wup_ref[:, cols])
        ubuf[A_CARRY:A_CARRY + t_rows, cols] = up
        w = cfw_ref[:, cols]
        return (w[0:1] * ubuf[A_CARRY - 2:A_CARRY - 2 + t_rows, cols]
                + w[1:2] * ubuf[A_CARRY - 1:A_CARRY - 1 + t_rows, cols]
                + w[2:3] * up)

    for c in range(D_FF // FF_CHUNK):
        gate = conv3(c * FF_CHUNK)
        val = conv3(D_FF + c * FF_CHUNK)
        act[:, c * FF_CHUNK:(c + 1) * FF_CHUNK] = (gate * jax.nn.sigmoid(gate) * val).astype(BF16)

    return x1 + _dot(act[...], wdn_ref[...])


def _ffn_kernel(*refs, tiles_per_seq):
    ffn_refs, (fg_ref, o_ref, ubuf, act) = refs[:10], refs[10:]
    t_rows = o_ref.shape[0]

    @pl.when(pl.program_id(0) % tiles_per_seq == 0)
    def _():
        ubuf[0:A_CARRY, :] = jnp.zeros((A_CARRY, 2 * D_FF), F32)

    x2 = _ffn_body(*ffn_refs, ubuf, act)
    ubuf[0:A_CARRY, :] = ubuf[t_rows:t_rows + A_CARRY, :]
    o_ref[...] = _rms(x2, fg_ref[...])


def _ffn_inproj_kernel(*refs, n_tiles, tiles_per_seq):
    ffn_refs, inproj_refs = refs[:10], refs[10:19]
    o_ref, ya_o, qkv_o, yc_o, ubuf, act, carry, xprev, tbuf, cbuf, ushift = refs[19:]
    t_rows = o_ref.shape[0]
    i = pl.program_id(0)
    tile = jnp.minimum(i, n_tiles - 1)
    slot_in = tile % 2

    @pl.when(i == 0)
    def _():
        xprev[...] = jnp.zeros(xprev.shape, F32)

    @pl.when(tile % tiles_per_seq == 0)
    def _():
        carry[slot_in] = jnp.zeros((A_CARRY, 2 * D_FF), F32)

    _inproj_body(xprev[...], jnp.logical_or(i == 0, (i - 1) % tiles_per_seq == 0), *inproj_refs,
                 ya_o, qkv_o, yc_o, tbuf, cbuf, ushift)

    ubuf[0:A_CARRY, :] = carry[slot_in]
    x2 = _ffn_body(*ffn_refs, ubuf, act)
    carry[1 - slot_in] = ubuf[t_rows:t_rows + A_CARRY, :]
    o_ref[...] = x2
    xprev[...] = x2


def _ffn_in_specs(layer, t_rows, tile_index):
    tile = lambda width: pl.BlockSpec((t_rows, width), lambda i: (tile_index(i), 0))
    row = lambda width: pl.BlockSpec((None, 1, width), lambda i: (layer, 0, 0))
    weight = lambda r, c: pl.BlockSpec((None, r, c), lambda i: (layer, 0, 0), pipeline_mode=pl.Buffered(1))
    return [tile(D_MODEL), tile(A_WIDTH), tile(B_WIDTH), tile(C_WIDTH),
            row(B_WIDTH), weight(D_MODEL, D_MODEL), row(D_MODEL), weight(D_MODEL, 2 * D_FF),
            pl.BlockSpec((None, FFN_CONV, 2 * D_FF), lambda i: (layer, 0, 0)),
            weight(D_FF, D_MODEL)]


def _ffn_scratch(t_rows):
    return [pltpu.VMEM((A_CARRY + t_rows, 2 * D_FF), F32), pltpu.VMEM((t_rows, D_FF), BF16)]


def _ffn_final(x, ya, yb, yc, layer, seq, ffn_params, fg):
    n = x.shape[0]
    return pl.pallas_call(
        functools.partial(_ffn_kernel, tiles_per_seq=seq // T_FFN),
        grid=(n // T_FFN,),
        in_specs=_ffn_in_specs(layer, T_FFN, lambda i: i) + [pl.BlockSpec((1, D_MODEL), lambda i: (0, 0))],
        out_specs=pl.BlockSpec((T_FFN, D_MODEL), lambda i: (i, 0)),
        out_shape=jax.ShapeDtypeStruct((n, D_MODEL), F32),
        scratch_shapes=_ffn_scratch(T_FFN),
        compiler_params=pltpu.CompilerParams(dimension_semantics=("arbitrary",), vmem_limit_bytes=VMEM_LIMIT),
        name="outproj_ffn",
    )(x, ya, yb, yc, *ffn_params, fg)


def _ffn_inproj(x, ya, yb, yc, layer, seq, ffn_params, inproj_params):
    n = x.shape[0]
    n_tiles = n // T_FUSED
    ffn_tile = lambda i: jnp.minimum(i, n_tiles - 1)
    inproj_specs, inproj_shapes = _inproj_outputs(n, seq, T_FUSED, lambda i: jnp.maximum(i - 1, 0))
    return pl.pallas_call(
        functools.partial(_ffn_inproj_kernel, n_tiles=n_tiles, tiles_per_seq=seq // T_FUSED),
        grid=(n_tiles + 1,),
        in_specs=_ffn_in_specs(layer, T_FUSED, ffn_tile) + _inproj_in_specs(layer + 1),
        out_specs=[pl.BlockSpec((T_FUSED, D_MODEL), lambda i: (ffn_tile(i), 0))] + inproj_specs,
        out_shape=[jax.ShapeDtypeStruct((n, D_MODEL), F32)] + inproj_shapes,
        scratch_shapes=(_ffn_scratch(T_FUSED)
                        + [pltpu.VMEM((2, A_CARRY, 2 * D_FF), F32), pltpu.VMEM((T_FUSED, D_MODEL), F32)]
                        + _inproj_scratch(T_FUSED)),
        compiler_params=pltpu.CompilerParams(dimension_semantics=("arbitrary",), vmem_limit_bytes=VMEM_LIMIT),
        name="ffn_next_inproj",
    )(x, ya, yb, yc, *ffn_params, *inproj_params)


def kernel(x, norm_mix_g, w_in, conv_a_w, conv_c_w, conv_c_b, ln_c_g, ln_c_b, out_norm_g, w_out, norm_ffn_g, w_up, conv_f_w, w_down, rel_bias, final_g):
    batch, seq, d_model = x.shape
    depth = w_in.shape[0]
    assert d_model == D_MODEL and seq % (BLK * max(DILATIONS)) == 0
    assert seq % T_IN == 0 and seq % T_FFN == 0 and seq % T_FUSED == 0
    assert w_in.shape[2] == IN_COLS and w_up.shape[2] == 2 * D_FF and D_FF % FF_CHUNK == 0

    rows = lambda a: a.reshape(depth, 1, a.shape[-1])
    w_in_b, w_out_b, w_up_b, w_dn_b = (w.astype(BF16) for w in (w_in, w_out, w_up, w_down))
    g_mix, g_ffn, ccb, lng, lnb = (rows(a) for a in (norm_mix_g, norm_ffn_g, conv_c_b, ln_c_g, ln_c_b))
    g_a = rows(out_norm_g[:, 0:A_WIDTH])
    g_b = rows(out_norm_g[:, A_WIDTH:A_WIDTH + B_WIDTH])
    g_c = rows(out_norm_g[:, A_WIDTH + B_WIDTH:])
    fg = final_g.reshape(1, D_MODEL)

    inproj_params = (g_mix, w_in_b, conv_a_w, conv_c_w, ccb, lng, lnb, g_a, g_c)
    ffn_params = (g_b, w_out_b, g_ffn, w_up_b, conv_f_w, w_dn_b)

    bias_tab = _bias_table(rel_bias)
    xf = x.reshape(batch * seq, D_MODEL)
    ya, qkv, yc = _inproj(xf, 0, seq, *inproj_params)
    for layer in range(depth):
        yb = _attention(qkv, bias_tab).reshape(batch * seq, B_WIDTH)
        if layer < depth - 1:
            xf, ya, qkv, yc = _ffn_inproj(xf, ya, yb, yc, layer, seq, ffn_params, inproj_params)
        else:
            xf = _ffn_final(xf, ya, yb, yc, layer, seq, ffn_params, fg)
    return xf.reshape(batch, seq, D_MODEL)
```

```python
import functools
import math

import numpy as np
import jax
import jax.numpy as jnp
from jax import lax
from jax.experimental import pallas as pl
from jax.experimental.pallas import tpu as pltpu

F32 = jnp.float32
BF16 = jnp.bfloat16

D_MODEL = 1024
D_HEAD = 64
A_WIDTH = 256
B_HEADS = 8
B_WIDTH = 512
C_WIDTH = 256
IN_COLS = 3 * A_WIDTH + 3 * B_WIDTH + 2 * C_WIDTH
QKV_LO = 3 * A_WIDTH
C_LO = QKV_LO + 3 * B_WIDTH
DILATIONS = (1, 4, 16)
N_KEYS = 128
BLK = 128
NUM_BUCKETS = 32
MAX_DISTANCE = 2048
SHORT_CONV = 3
CONFORMER_CONV = 31
FFN_CONV = 3
D_FF = 2816
EPS = 1e-6
NEG = -1e30
LOG2E = math.log2(math.e)

LANES = 128
SUBLANES = 8
VMEM_LIMIT = 56 * 1024 * 1024

T_IN = 512
T_FFN = 512
CONV_ROWS = 64
FF_CHUNK = 256
A_CARRY = SUBLANES
C_CARRY = 32


def _rms(x, g):
    return x * lax.rsqrt(jnp.mean(x * x, axis=-1, keepdims=True) + EPS) * g


def _dot(a, b):
    return jnp.dot(a, b, preferred_element_type=F32)


def _bucket_table():
    rel = np.arange(BLK)[:, None] - np.arange(2 * BLK)[None, :] + BLK
    valid = (rel >= 0) & (rel <= N_KEYS)
    max_exact = NUM_BUCKETS // 2
    out = []
    for d in DILATIONS:
        dist = np.maximum(rel, 0) * d
        d_f = np.maximum(dist, 1).astype(np.float32)
        large = max_exact + (np.log(d_f / np.float32(max_exact)) / np.float32(math.log(MAX_DISTANCE / max_exact))
                             * np.float32(NUM_BUCKETS - max_exact)).astype(np.int32)
        large = np.minimum(large, NUM_BUCKETS - 1)
        bucket = np.where(dist < max_exact, dist, large)
        out.append(np.where(valid, bucket, -1).astype(np.int32))
    return np.stack(out)


def _bias_kernel(rb_ref, bucket_ref, o_ref):
    h = pl.program_id(1)
    bucket = bucket_ref[...]
    acc = jnp.full(bucket.shape, NEG, F32)
    for b in range(NUM_BUCKETS):
        acc = jnp.where(bucket == b, rb_ref[b, h] * LOG2E, acc)
    o_ref[...] = acc


def _bias_table(rel_bias):
    buckets = jnp.asarray(_bucket_table())
    return pl.pallas_call(
        _bias_kernel,
        grid=(len(DILATIONS), B_HEADS),
        in_specs=[pl.BlockSpec(memory_space=pltpu.SMEM),
                  pl.BlockSpec((None, BLK, 2 * BLK), lambda br, h: (br, 0, 0))],
        out_specs=pl.BlockSpec((None, None, BLK, 2 * BLK), lambda br, h: (br, h, 0, 0)),
        out_shape=jax.ShapeDtypeStruct((len(DILATIONS), B_HEADS, BLK, 2 * BLK), F32),
        name="bias_table",
    )(rel_bias, buckets)


def _inproj_kernel(x_ref, g_ref, w_ref, caw_ref, ccw_ref, ccb_ref, lng_ref, lnb_ref, ga_ref, gc_ref,
                   ya_ref, qkv_ref, yc_ref, tbuf, ubuf, ushift, *, tiles_per_seq):
    t_rows = x_ref.shape[0]

    @pl.when(pl.program_id(0) % tiles_per_seq == 0)
    def _():
        tbuf[0:A_CARRY, :] = jnp.zeros((A_CARRY, A_WIDTH), F32)
        ubuf[0:C_CARRY, :] = jnp.zeros((C_CARRY, C_WIDTH), F32)

    h = _rms(x_ref[...], g_ref[...]).astype(BF16)

    zc = _dot(h, w_ref[:, C_LO:IN_COLS])
    ubuf[C_CARRY:C_CARRY + t_rows, :] = zc[:, 0:C_WIDTH] * jax.nn.sigmoid(zc[:, C_WIDTH:2 * C_WIDTH])
    ccw = ccw_ref[...]
    first = C_CARRY - (CONFORMER_CONV - 1)
    shift_rows = C_CARRY + t_rows - SUBLANES
    for ph in range(1, SUBLANES):
        ushift[ph - 1, :, :] = ubuf[pl.ds(ph, shift_rows), :]
    for rc in range(t_rows // CONV_ROWS):
        acc = jnp.broadcast_to(ccb_ref[...], (CONV_ROWS, C_WIDTH))
        for k in range(CONFORMER_CONV):
            tiles, ph = divmod(first + k, SUBLANES)
            rows = pl.ds(rc * CONV_ROWS + tiles * SUBLANES, CONV_ROWS)
            acc = acc + ccw[k:k + 1] * (ubuf[rows, :] if ph == 0 else ushift[ph - 1, rows, :])
        mu = jnp.mean(acc, axis=-1, keepdims=True)
        cen = acc - mu
        var = jnp.mean(cen * cen, axis=-1, keepdims=True)
        ln = cen * lax.rsqrt(var + EPS) * lng_ref[...] + lnb_ref[...]
        yc = ln * jax.nn.sigmoid(ln)
        yc_ref[rc * CONV_ROWS:(rc + 1) * CONV_ROWS, :] = _rms(yc, gc_ref[...]).astype(BF16)
    ubuf[0:C_CARRY, :] = ubuf[t_rows:t_rows + C_CARRY, :]

    za = _dot(h, w_ref[:, 0:QKV_LO])
    a_h, a_b, a_c = za[:, 0:A_WIDTH], za[:, A_WIDTH:2 * A_WIDTH], za[:, 2 * A_WIDTH:3 * A_WIDTH]
    t = a_c * a_h
    tbuf[A_CARRY:A_CARRY + t_rows, :] = t
    caw = caw_ref[...]
    conv = (caw[0:1] * tbuf[A_CARRY - 2:A_CARRY - 2 + t_rows, :]
            + caw[1:2] * tbuf[A_CARRY - 1:A_CARRY - 1 + t_rows, :]
            + caw[2:3] * t)
    tbuf[0:A_CARRY, :] = tbuf[t_rows:t_rows + A_CARRY, :]
    ya_ref[...] = _rms(a_b * conv, ga_ref[...]).astype(BF16)

    zqkv = _dot(h, w_ref[:, QKV_LO:C_LO])
    pairs = B_WIDTH // LANES
    for g in range(3 * pairs):
        slab = zqkv[:, g * LANES:(g + 1) * LANES]
        qkv_ref[g, :, :] = slab * (D_HEAD ** -0.5 * LOG2E) if g < pairs else slab


def _inproj(x, layer, seq, g, w_in, caw, ccw, ccb, lng, lnb, ga, gc):
    n = x.shape[0]
    tiles_per_seq = seq // T_IN
    slabs = 3 * B_WIDTH // LANES
    row = lambda width: pl.BlockSpec((None, 1, width), lambda i: (layer, 0, 0))
    return pl.pallas_call(
        functools.partial(_inproj_kernel, tiles_per_seq=tiles_per_seq),
        grid=(n // T_IN,),
        in_specs=[pl.BlockSpec((T_IN, D_MODEL), lambda i: (i, 0)),
                  row(D_MODEL),
                  pl.BlockSpec((None, D_MODEL, IN_COLS), lambda i: (layer, 0, 0), pipeline_mode=pl.Buffered(1)),
                  pl.BlockSpec((None, SHORT_CONV, A_WIDTH), lambda i: (layer, 0, 0)),
                  pl.BlockSpec((None, CONFORMER_CONV, C_WIDTH), lambda i: (layer, 0, 0)),
                  row(C_WIDTH), row(C_WIDTH), row(C_WIDTH), row(A_WIDTH), row(C_WIDTH)],
        out_specs=[pl.BlockSpec((T_IN, A_WIDTH), lambda i: (i, 0)),
                   pl.BlockSpec((None, slabs, T_IN, LANES),
                                lambda i: (i // tiles_per_seq, 0, i % tiles_per_seq, 0)),
                   pl.BlockSpec((T_IN, C_WIDTH), lambda i: (i, 0))],
        out_shape=[jax.ShapeDtypeStruct((n, A_WIDTH), BF16),
                   jax.ShapeDtypeStruct((n // seq, slabs, seq, LANES), F32),
                   jax.ShapeDtypeStruct((n, C_WIDTH), BF16)],
        scratch_shapes=[pltpu.VMEM((A_CARRY + T_IN, A_WIDTH), F32),
                        pltpu.VMEM((C_CARRY + T_IN, C_WIDTH), F32),
                        pltpu.VMEM((SUBLANES - 1, C_CARRY + T_IN - SUBLANES, C_WIDTH), F32)],
        compiler_params=pltpu.CompilerParams(dimension_semantics=("arbitrary",), vmem_limit_bytes=VMEM_LIMIT),
        name="inproj_mix",
    )(x, g, w_in, caw, ccw, ccb, lng, lnb, ga, gc)


def _attn_kernel(q_ref, k_ref, v_ref, bias_ref, o_ref, g4q, g4k, g4v, qs, ks, v0s, v1s, *stats):
    seq = q_ref.shape[0]
    n_blocks = seq // BLK
    quarter = seq // 4
    low = lax.broadcasted_iota(jnp.int32, (1, LANES), 1) < D_HEAD
    nt = (((1,), (1,)), ((), ()))
    head_lanes = (low, jnp.logical_not(low))
    v_ones = (v0s, v1s)

    def fill(dst, qv, kv, vv):
        qs[dst, :] = qv.astype(BF16)
        ks[dst, :] = kv.astype(BF16)
        v0s[dst, :] = jnp.where(low, vv, 1.0).astype(BF16)
        v1s[dst, :] = jnp.where(low, 1.0, vv).astype(BF16)

    for br, d in enumerate(DILATIONS):
        if d == 1:
            for c in range(4):
                rows = pl.ds(c * quarter, quarter)
                fill(rows, q_ref[rows, :], k_ref[rows, :], v_ref[rows, :])
        elif d == 4:
            for r in range(4):
                src, dst = pl.ds(r, quarter, stride=4), pl.ds(r * quarter, quarter)
                qv, kv, vv = q_ref[src, :], k_ref[src, :], v_ref[src, :]
                g4q[dst, :], g4k[dst, :], g4v[dst, :] = qv, kv, vv
                fill(dst, qv, kv, vv)
        else:
            for r in range(16):
                src, dst = pl.ds((r % 4) * quarter + r // 4, BLK, stride=4), pl.ds(r * BLK, BLK)
                fill(dst, g4q[src, :], g4k[src, :], g4v[src, :])

        per_class = seq // d // BLK
        blocks = []
        for b in range(n_blocks):
            r, j = divmod(b, per_class)
            rows = pl.ds(b * BLK, BLK)
            keys = rows if j == 0 else pl.ds((b - 1) * BLK, 2 * BLK)
            cols = pl.ds(BLK, BLK) if j == 0 else pl.ds(0, 2 * BLK)
            out = pl.ds((r % 4) * quarter + r // 4, BLK, stride=4) if d == 16 else rows
            blocks.append((rows, keys, cols, out))

        scores = []
        for rows, keys, cols, out in blocks:
            for hh in range(2):
                q_head = jnp.where(head_lanes[hh], qs[rows, :], jnp.zeros((), BF16))
                scores.append(lax.dot_general(q_head, ks[keys, :], nt, preferred_element_type=F32)
                              + bias_ref[br, hh, :, cols])
        probs = []
        for i, (rows, keys, cols, out) in enumerate(blocks):
            for hh in range(2):
                s = scores[2 * i + hh]
                m = jnp.max(s, axis=-1, keepdims=True)
                probs.append(jnp.exp2(s - m).astype(BF16))
                stats[2 * (2 * br + hh) + 1][out, :] = jnp.broadcast_to(m, (BLK, LANES))
        for i, (rows, keys, cols, out) in enumerate(blocks):
            for hh in range(2):
                stats[2 * (2 * br + hh)][out, :] = _dot(probs[2 * i + hh], v_ones[hh][keys, :])

    for t in range(n_blocks):
        r4, n = divmod(t, 4)
        rows4 = pl.ds(t * BLK, BLK)
        nat = pl.ds(4 * BLK * n + r4, BLK, stride=4)
        y = []
        for hh in range(2):
            where = (nat, rows4, rows4)
            ms = [stats[2 * (2 * br + hh) + 1][where[br], :] for br in range(len(DILATIONS))]
            m_all = jnp.maximum(jnp.maximum(ms[0], ms[1]), ms[2])
            tot = jnp.zeros((BLK, LANES), F32)
            for br in range(len(DILATIONS)):
                tot = tot + jnp.exp2(ms[br] - m_all) * stats[2 * (2 * br + hh)][where[br], :]
            y.append(tot / pltpu.roll(tot, D_HEAD, axis=1))
        o_ref[nat, :] = jnp.where(low, y[0], y[1])


def _attention(qkv, bias_tab):
    batch, _, seq, _ = qkv.shape
    pairs = B_WIDTH // LANES
    n_br = len(DILATIONS)
    blk = lambda off: pl.BlockSpec((None, None, seq, LANES), lambda g, b: (b, off * pairs + g, 0, 0))
    return pl.pallas_call(
        _attn_kernel,
        grid=(pairs, batch),
        in_specs=[blk(0), blk(1), blk(2),
                  pl.BlockSpec((n_br, 2, BLK, 2 * BLK), lambda g, b: (0, g, 0, 0))],
        out_specs=pl.BlockSpec((None, seq, LANES), lambda g, b: (b, 0, g)),
        out_shape=jax.ShapeDtypeStruct((batch, seq, B_WIDTH), F32),
        scratch_shapes=([pltpu.VMEM((seq, LANES), F32)] * 3 + [pltpu.VMEM((seq, LANES), BF16)] * 4
                        + [pltpu.VMEM((seq, LANES), F32)] * (4 * n_br)),
        compiler_params=pltpu.CompilerParams(dimension_semantics=("arbitrary", "arbitrary"),
                                             vmem_limit_bytes=VMEM_LIMIT),
        name="dilated_attn",
    )(qkv, qkv, qkv, bias_tab)


def _ffn_kernel(x_ref, ya_ref, yb_ref, yc_ref, gb_ref, wout_ref, gf_ref, wup_ref, cfw_ref, wdn_ref, fg_ref,
                o_ref, carry, act, *, tiles_per_seq, final):
    t_rows = x_ref.shape[0]

    @pl.when(pl.program_id(0) % tiles_per_seq == 0)
    def _():
        carry[...] = jnp.zeros((A_CARRY, 2 * D_FF), F32)

    y = jnp.concatenate([ya_ref[...], _rms(yb_ref[...], gb_ref[...]).astype(BF16), yc_ref[...]], axis=-1)
    x1 = x_ref[...] + _dot(y, wout_ref[...])
    h = _rms(x1, gf_ref[...]).astype(BF16)

    def conv3(c0):
        cols = pl.ds(c0, FF_CHUNK)
        up = _dot(h, wup_ref[:, cols])
        ext = jnp.concatenate([carry[:, cols], up], axis=0)
        carry[:, cols] = up[t_rows - A_CARRY:, :]
        w = cfw_ref[:, cols]
        return (w[0:1] * pltpu.roll(ext, 2, axis=0)[A_CARRY:, :]
                + w[1:2] * pltpu.roll(ext, 1, axis=0)[A_CARRY:, :]
                + w[2:3] * up)

    for c in range(D_FF // FF_CHUNK):
        gate = conv3(c * FF_CHUNK)
        val = conv3(D_FF + c * FF_CHUNK)
        act[:, c * FF_CHUNK:(c + 1) * FF_CHUNK] = (gate * jax.nn.sigmoid(gate) * val).astype(BF16)

    x2 = x1 + _dot(act[...], wdn_ref[...])
    if final:
        x2 = _rms(x2, fg_ref[...])
    o_ref[...] = x2


def _ffn(x, ya, yb, yc, layer, seq, final, gb, w_out, gf, w_up, cfw, w_dn, fg):
    n = x.shape[0]
    tile = lambda width: pl.BlockSpec((T_FFN, width), lambda i: (i, 0))
    row = lambda width: pl.BlockSpec((None, 1, width), lambda i: (layer, 0, 0))
    weight = lambda r, c: pl.BlockSpec((None, r, c), lambda i: (layer, 0, 0), pipeline_mode=pl.Buffered(1))
    return pl.pallas_call(
        functools.partial(_ffn_kernel, tiles_per_seq=seq // T_FFN, final=final),
        grid=(n // T_FFN,),
        in_specs=[tile(D_MODEL), tile(A_WIDTH), tile(B_WIDTH), tile(C_WIDTH),
                  row(B_WIDTH), weight(D_MODEL, D_MODEL), row(D_MODEL), weight(D_MODEL, 2 * D_FF),
                  pl.BlockSpec((None, FFN_CONV, 2 * D_FF), lambda i: (layer, 0, 0)),
                  weight(D_FF, D_MODEL),
                  pl.BlockSpec((1, D_MODEL), lambda i: (0, 0))],
        out_specs=tile(D_MODEL),
        out_shape=jax.ShapeDtypeStruct((n, D_MODEL), F32),
        scratch_shapes=[pltpu.VMEM((A_CARRY, 2 * D_FF), F32),
                        pltpu.VMEM((T_FFN, D_FF), BF16)],
        compiler_params=pltpu.CompilerParams(dimension_semantics=("arbitrary",), vmem_limit_bytes=VMEM_LIMIT),
        name="outproj_ffn",
    )(x, ya, yb, yc, gb, w_out, gf, w_up, cfw, w_dn, fg)


def kernel(x, norm_mix_g, w_in, conv_a_w, conv_c_w, conv_c_b, ln_c_g, ln_c_b, out_norm_g, w_out, norm_ffn_g, w_up, conv_f_w, w_down, rel_bias, final_g):
    batch, seq, d_model = x.shape
    depth = w_in.shape[0]
    assert d_model == D_MODEL and seq % T_IN == 0 and seq % T_FFN == 0 and seq % (BLK * max(DILATIONS)) == 0
    assert w_in.shape[2] == IN_COLS and w_up.shape[2] == 2 * D_FF and D_FF % FF_CHUNK == 0

    rows = lambda a: a.reshape(depth, 1, a.shape[-1])
    w_in_b, w_out_b, w_up_b, w_dn_b = (w.astype(BF16) for w in (w_in, w_out, w_up, w_down))
    g_mix, g_ffn, ccb, lng, lnb = (rows(a) for a in (norm_mix_g, norm_ffn_g, conv_c_b, ln_c_g, ln_c_b))
    g_a = rows(out_norm_g[:, 0:A_WIDTH])
    g_b = rows(out_norm_g[:, A_WIDTH:A_WIDTH + B_WIDTH])
    g_c = rows(out_norm_g[:, A_WIDTH + B_WIDTH:])
    fg = final_g.reshape(1, D_MODEL)

    bias_tab = _bias_table(rel_bias)
    xf = x.reshape(batch * seq, D_MODEL)
    for layer in range(depth):
        ya, qkv, yc = _inproj(xf, layer, seq, g_mix, w_in_b, conv_a_w, conv_c_w, ccb, lng, lnb, g_a, g_c)
        yb = _attention(qkv, bias_tab).reshape(batch * seq, B_WIDTH)
        xf = _ffn(xf, ya, yb, yc, layer, seq, layer == depth - 1, g_b, w_out_b, g_ffn, w_up_b, conv_f_w, w_dn_b, fg)
    return xf.reshape(batch, seq, D_MODEL)
```

```python
import functools
import math

import numpy as np
import jax
import jax.numpy as jnp
from jax import lax
from jax.experimental import pallas as pl
from jax.experimental.pallas import tpu as pltpu

F32 = jnp.float32
BF16 = jnp.bfloat16

D_MODEL = 1024
D_HEAD = 64
A_WIDTH = 256
B_HEADS = 8
B_WIDTH = 512
C_WIDTH = 256
IN_COLS = 3 * A_WIDTH + 3 * B_WIDTH + 2 * C_WIDTH
QKV_LO = 3 * A_WIDTH
C_LO = QKV_LO + 3 * B_WIDTH
DILATIONS = (1, 4, 16)
N_KEYS = 128
BLK = 128
NUM_BUCKETS = 32
MAX_DISTANCE = 2048
SHORT_CONV = 3
CONFORMER_CONV = 31
FFN_CONV = 3
D_FF = 2816
EPS = 1e-6
NEG = -1e30
LOG2E = math.log2(math.e)

LANES = 128
SUBLANES = 8
COMPILER_VMEM = 8 * 1024 * 1024

T_IN = 512
T_FFN = 512
CONV_ROWS = 64
FF_CHUNK = 256
A_CARRY = SUBLANES
C_CARRY = 32
BIAS_ROWS = 16


def _nbytes(shape, dtype):
    return math.prod(shape) * jnp.dtype(dtype).itemsize


def _vmem_limit(streamed, resident, scratch):
    total = 2 * sum(_nbytes(*b) for b in streamed) + sum(_nbytes(*b) for b in resident + scratch)
    return total + COMPILER_VMEM


def _rms(x, g):
    return x * lax.rsqrt(jnp.mean(x * x, axis=-1, keepdims=True) + EPS) * g


def _dot(a, b):
    return jnp.dot(a, b, preferred_element_type=F32)


def _bucket_table():
    rel = np.arange(BLK)[:, None] - np.arange(2 * BLK)[None, :] + BLK
    valid = (rel >= 0) & (rel <= N_KEYS)
    max_exact = NUM_BUCKETS // 2
    out = []
    for d in DILATIONS:
        dist = np.maximum(rel, 0) * d
        d_f = np.maximum(dist, 1).astype(np.float32)
        large = max_exact + (np.log(d_f / np.float32(max_exact)) / np.float32(math.log(MAX_DISTANCE / max_exact))
                             * np.float32(NUM_BUCKETS - max_exact)).astype(np.int32)
        large = np.minimum(large, NUM_BUCKETS - 1)
        bucket = np.where(dist < max_exact, dist, large)
        out.append(np.where(valid, bucket, -1).astype(np.int32))
    return np.stack(out)


def _bias_kernel(rb_ref, bucket_ref, o_ref):
    for r0 in range(0, BLK, BIAS_ROWS):
        bucket = bucket_ref[r0:r0 + BIAS_ROWS, :]
        acc = [jnp.full(bucket.shape, NEG, F32) for _ in range(B_HEADS)]
        for b in range(NUM_BUCKETS):
            hit = bucket == b
            acc = [jnp.where(hit, rb_ref[b, h] * LOG2E, acc[h]) for h in range(B_HEADS)]
        for h in range(B_HEADS):
            o_ref[h, r0:r0 + BIAS_ROWS, :] = acc[h]


def _bias_table(rel_bias):
    buckets = jnp.asarray(_bucket_table())
    return pl.pallas_call(
        _bias_kernel,
        grid=(len(DILATIONS),),
        in_specs=[pl.BlockSpec(memory_space=pltpu.SMEM),
                  pl.BlockSpec((None, BLK, 2 * BLK), lambda br: (br, 0, 0))],
        out_specs=pl.BlockSpec((None, B_HEADS, BLK, 2 * BLK), lambda br: (br, 0, 0, 0)),
        out_shape=jax.ShapeDtypeStruct((len(DILATIONS), B_HEADS, BLK, 2 * BLK), F32),
        name="bias_table",
    )(rel_bias, buckets)


def _inproj_kernel(x_ref, g_ref, w_ref, caw_ref, ccw_ref, ccb_ref, lng_ref, lnb_ref, ga_ref, gc_ref,
                   ya_ref, qkv_ref, yc_ref, tbuf, ubuf, ushift, *, tiles_per_seq):
    t_rows = x_ref.shape[0]

    @pl.when(pl.program_id(0) % tiles_per_seq == 0)
    def _():
        tbuf[0:A_CARRY, :] = jnp.zeros((A_CARRY, A_WIDTH), F32)
        ubuf[0:C_CARRY, :] = jnp.zeros((C_CARRY, C_WIDTH), F32)

    h = _rms(x_ref[...], g_ref[...]).astype(BF16)

    zc = _dot(h, w_ref[:, C_LO:IN_COLS])
    ubuf[C_CARRY:C_CARRY + t_rows, :] = zc[:, 0:C_WIDTH] * jax.nn.sigmoid(zc[:, C_WIDTH:2 * C_WIDTH])
    ccw = ccw_ref[...]
    first = C_CARRY - (CONFORMER_CONV - 1)
    shift_rows = C_CARRY + t_rows - SUBLANES
    for ph in range(1, SUBLANES):
        ushift[ph - 1, :, :] = ubuf[pl.ds(ph, shift_rows), :]
    for rc in range(t_rows // CONV_ROWS):
        acc = jnp.broadcast_to(ccb_ref[...], (CONV_ROWS, C_WIDTH))
        for k in range(CONFORMER_CONV):
            tiles, ph = divmod(first + k, SUBLANES)
            rows = pl.ds(rc * CONV_ROWS + tiles * SUBLANES, CONV_ROWS)
            acc = acc + ccw[k:k + 1] * (ubuf[rows, :] if ph == 0 else ushift[ph - 1, rows, :])
        mu = jnp.mean(acc, axis=-1, keepdims=True)
        cen = acc - mu
        var = jnp.mean(cen * cen, axis=-1, keepdims=True)
        ln = cen * lax.rsqrt(var + EPS) * lng_ref[...] + lnb_ref[...]
        yc = ln * jax.nn.sigmoid(ln)
        yc_ref[rc * CONV_ROWS:(rc + 1) * CONV_ROWS, :] = _rms(yc, gc_ref[...]).astype(BF16)
    ubuf[0:C_CARRY, :] = ubuf[t_rows:t_rows + C_CARRY, :]

    za = _dot(h, w_ref[:, 0:QKV_LO])
    a_h, a_b, a_c = za[:, 0:A_WIDTH], za[:, A_WIDTH:2 * A_WIDTH], za[:, 2 * A_WIDTH:3 * A_WIDTH]
    t = a_c * a_h
    tbuf[A_CARRY:A_CARRY + t_rows, :] = t
    caw = caw_ref[...]
    conv = (caw[0:1] * tbuf[A_CARRY - 2:A_CARRY - 2 + t_rows, :]
            + caw[1:2] * tbuf[A_CARRY - 1:A_CARRY - 1 + t_rows, :]
            + caw[2:3] * t)
    tbuf[0:A_CARRY, :] = tbuf[t_rows:t_rows + A_CARRY, :]
    ya_ref[...] = _rms(a_b * conv, ga_ref[...]).astype(BF16)

    zqkv = _dot(h, w_ref[:, QKV_LO:C_LO])
    pairs = B_WIDTH // LANES
    for g in range(3 * pairs):
        slab = zqkv[:, g * LANES:(g + 1) * LANES]
        qkv_ref[g, :, :] = slab * (D_HEAD ** -0.5 * LOG2E) if g < pairs else slab


def _inproj(x, layer, seq, g, w_in, caw, ccw, ccb, lng, lnb, ga, gc):
    n = x.shape[0]
    tiles_per_seq = seq // T_IN
    slabs = 3 * B_WIDTH // LANES
    row = lambda width: pl.BlockSpec((None, 1, width), lambda i: (layer, 0, 0))
    scratch = [((A_CARRY + T_IN, A_WIDTH), F32), ((C_CARRY + T_IN, C_WIDTH), F32),
               ((SUBLANES - 1, C_CARRY + T_IN - SUBLANES, C_WIDTH), F32)]
    streamed = [((T_IN, D_MODEL), F32), ((T_IN, A_WIDTH), BF16), ((slabs, T_IN, LANES), F32), ((T_IN, C_WIDTH), BF16)]
    return pl.pallas_call(
        functools.partial(_inproj_kernel, tiles_per_seq=tiles_per_seq),
        grid=(n // T_IN,),
        in_specs=[pl.BlockSpec((T_IN, D_MODEL), lambda i: (i, 0)),
                  row(D_MODEL),
                  pl.BlockSpec((None, D_MODEL, IN_COLS), lambda i: (layer, 0, 0), pipeline_mode=pl.Buffered(1)),
                  pl.BlockSpec((None, SHORT_CONV, A_WIDTH), lambda i: (layer, 0, 0)),
                  pl.BlockSpec((None, CONFORMER_CONV, C_WIDTH), lambda i: (layer, 0, 0)),
                  row(C_WIDTH), row(C_WIDTH), row(C_WIDTH), row(A_WIDTH), row(C_WIDTH)],
        out_specs=[pl.BlockSpec((T_IN, A_WIDTH), lambda i: (i, 0)),
                   pl.BlockSpec((None, slabs, T_IN, LANES),
                                lambda i: (i // tiles_per_seq, 0, i % tiles_per_seq, 0)),
                   pl.BlockSpec((T_IN, C_WIDTH), lambda i: (i, 0))],
        out_shape=[jax.ShapeDtypeStruct((n, A_WIDTH), BF16),
                   jax.ShapeDtypeStruct((n // seq, slabs, seq, LANES), F32),
                   jax.ShapeDtypeStruct((n, C_WIDTH), BF16)],
        scratch_shapes=[pltpu.VMEM(*s) for s in scratch],
        compiler_params=pltpu.CompilerParams(
            dimension_semantics=("arbitrary",),
            vmem_limit_bytes=_vmem_limit(streamed, [((D_MODEL, IN_COLS), BF16)], scratch)),
        name="inproj_mix",
    )(x, g, w_in, caw, ccw, ccb, lng, lnb, ga, gc)


def _attn_kernel(q_ref, k_ref, v_ref, bias_ref, o_ref, g4q, g4k, g4v, qs, ks, v0s, v1s, *stats):
    seq = q_ref.shape[0]
    n_blocks = seq // BLK
    quarter = seq // 4
    low = lax.broadcasted_iota(jnp.int32, (1, LANES), 1) < D_HEAD
    nt = (((1,), (1,)), ((), ()))
    head_lanes = (low, jnp.logical_not(low))
    v_ones = (v0s, v1s)

    def fill(dst, qv, kv, vv):
        qs[dst, :] = qv.astype(BF16)
        ks[dst, :] = kv.astype(BF16)
        v0s[dst, :] = jnp.where(low, vv, 1.0).astype(BF16)
        v1s[dst, :] = jnp.where(low, 1.0, vv).astype(BF16)

    for br, d in enumerate(DILATIONS):
        if d == 1:
            for c in range(4):
                rows = pl.ds(c * quarter, quarter)
                fill(rows, q_ref[rows, :], k_ref[rows, :], v_ref[rows, :])
        elif d == 4:
            for r in range(4):
                src, dst = pl.ds(r, quarter, stride=4), pl.ds(r * quarter, quarter)
                qv, kv, vv = q_ref[src, :], k_ref[src, :], v_ref[src, :]
                g4q[dst, :], g4k[dst, :], g4v[dst, :] = qv, kv, vv
                fill(dst, qv, kv, vv)
        else:
            for r in range(16):
                src, dst = pl.ds((r % 4) * quarter + r // 4, BLK, stride=4), pl.ds(r * BLK, BLK)
                fill(dst, g4q[src, :], g4k[src, :], g4v[src, :])

        per_class = seq // d // BLK
        blocks = []
        for b in range(n_blocks):
            r, j = divmod(b, per_class)
            rows = pl.ds(b * BLK, BLK)
            keys = rows if j == 0 else pl.ds((b - 1) * BLK, 2 * BLK)
            cols = pl.ds(BLK, BLK) if j == 0 else pl.ds(0, 2 * BLK)
            out = pl.ds((r % 4) * quarter + r // 4, BLK, stride=4) if d == 16 else rows
            blocks.append((rows, keys, cols, out))

        scores = []
        for rows, keys, cols, out in blocks:
            for hh in range(2):
                q_head = jnp.where(head_lanes[hh], qs[rows, :], jnp.zeros((), BF16))
                scores.append(lax.dot_general(q_head, ks[keys, :], nt, preferred_element_type=F32)
                              + bias_ref[br, hh, :, cols])
        probs = []
        for i, (rows, keys, cols, out) in enumerate(blocks):
            for hh in range(2):
                s = scores[2 * i + hh]
                m = jnp.max(s, axis=-1, keepdims=True)
                probs.append(jnp.exp2(s - m).astype(BF16))
                stats[2 * (2 * br + hh) + 1][out, :] = jnp.broadcast_to(m, (BLK, LANES))
        for i, (rows, keys, cols, out) in enumerate(blocks):
            for hh in range(2):
                stats[2 * (2 * br + hh)][out, :] = _dot(probs[2 * i + hh], v_ones[hh][keys, :])

    for t in range(n_blocks):
        r4, n = divmod(t, 4)
        rows4 = pl.ds(t * BLK, BLK)
        nat = pl.ds(4 * BLK * n + r4, BLK, stride=4)
        y = []
        for hh in range(2):
            where = (nat, rows4, rows4)
            ms = [stats[2 * (2 * br + hh) + 1][where[br], :] for br in range(len(DILATIONS))]
            m_all = jnp.maximum(jnp.maximum(ms[0], ms[1]), ms[2])
            tot = jnp.zeros((BLK, LANES), F32)
            for br in range(len(DILATIONS)):
                tot = tot + jnp.exp2(ms[br] - m_all) * stats[2 * (2 * br + hh)][where[br], :]
            y.append(tot / pltpu.roll(tot, D_HEAD, axis=1))
        o_ref[nat, :] = jnp.where(low, y[0], y[1])


def _attention(qkv, bias_tab):
    batch, _, seq, _ = qkv.shape
    pairs = B_WIDTH // LANES
    n_br = len(DILATIONS)
    blk = lambda off: pl.BlockSpec((None, None, seq, LANES), lambda g, b: (b, off * pairs + g, 0, 0))
    scratch = [((seq, LANES), F32)] * 3 + [((seq, LANES), BF16)] * 4 + [((seq, LANES), F32)] * (4 * n_br)
    streamed = [((seq, LANES), F32)] * 4 + [((n_br, 2, BLK, 2 * BLK), F32)]
    return pl.pallas_call(
        _attn_kernel,
        grid=(pairs, batch),
        in_specs=[blk(0), blk(1), blk(2),
                  pl.BlockSpec((n_br, 2, BLK, 2 * BLK), lambda g, b: (0, g, 0, 0))],
        out_specs=pl.BlockSpec((None, seq, LANES), lambda g, b: (b, 0, g)),
        out_shape=jax.ShapeDtypeStruct((batch, seq, B_WIDTH), F32),
        scratch_shapes=[pltpu.VMEM(*s) for s in scratch],
        compiler_params=pltpu.CompilerParams(dimension_semantics=("arbitrary", "arbitrary"),
                                             vmem_limit_bytes=_vmem_limit(streamed, [], scratch)),
        name="dilated_attn",
    )(qkv, qkv, qkv, bias_tab)


def _ffn_kernel(x_ref, ya_ref, yb_ref, yc_ref, gb_ref, wout_ref, gf_ref, wup_ref, cfw_ref, wdn_ref, fg_ref,
                o_ref, carry, act, *, tiles_per_seq, final):
    t_rows = x_ref.shape[0]

    @pl.when(pl.program_id(0) % tiles_per_seq == 0)
    def _():
        carry[...] = jnp.zeros((A_CARRY, 2 * D_FF), F32)

    y = jnp.concatenate([ya_ref[...], _rms(yb_ref[...], gb_ref[...]).astype(BF16), yc_ref[...]], axis=-1)
    x1 = x_ref[...] + _dot(y, wout_ref[...])
    h = _rms(x1, gf_ref[...]).astype(BF16)

    def conv3(c0):
        cols = pl.ds(c0, FF_CHUNK)
        up = _dot(h, wup_ref[:, cols])
        ext = jnp.concatenate([carry[:, cols], up], axis=0)
        carry[:, cols] = up[t_rows - A_CARRY:, :]
        w = cfw_ref[:, cols]
        return (w[0:1] * pltpu.roll(ext, 2, axis=0)[A_CARRY:, :]
                + w[1:2] * pltpu.roll(ext, 1, axis=0)[A_CARRY:, :]
                + w[2:3] * up)

    for c in range(D_FF // FF_CHUNK):
        gate = conv3(c * FF_CHUNK)
        val = conv3(D_FF + c * FF_CHUNK)
        act[:, c * FF_CHUNK:(c + 1) * FF_CHUNK] = (gate * jax.nn.sigmoid(gate) * val).astype(BF16)

    x2 = x1 + _dot(act[...], wdn_ref[...])
    if final:
        x2 = _rms(x2, fg_ref[...])
    o_ref[...] = x2


def _ffn(x, ya, yb, yc, layer, seq, final, gb, w_out, gf, w_up, cfw, w_dn, fg):
    n = x.shape[0]
    tile = lambda width: pl.BlockSpec((T_FFN, width), lambda i: (i, 0))
    row = lambda width: pl.BlockSpec((None, 1, width), lambda i: (layer, 0, 0))
    weight = lambda r, c: pl.BlockSpec((None, r, c), lambda i: (layer, 0, 0), pipeline_mode=pl.Buffered(1))
    scratch = [((A_CARRY, 2 * D_FF), F32), ((T_FFN, D_FF), BF16)]
    streamed = [((T_FFN, D_MODEL), F32)] * 2 + [((T_FFN, A_WIDTH), BF16), ((T_FFN, B_WIDTH), F32),
                                                ((T_FFN, C_WIDTH), BF16)]
    resident = [((D_MODEL, D_MODEL), BF16), ((D_MODEL, 2 * D_FF), BF16), ((D_FF, D_MODEL), BF16)]
    return pl.pallas_call(
        functools.partial(_ffn_kernel, tiles_per_seq=seq // T_FFN, final=final),
        grid=(n // T_FFN,),
        in_specs=[tile(D_MODEL), tile(A_WIDTH), tile(B_WIDTH), tile(C_WIDTH),
                  row(B_WIDTH), weight(D_MODEL, D_MODEL), row(D_MODEL), weight(D_MODEL, 2 * D_FF),
                  pl.BlockSpec((None, FFN_CONV, 2 * D_FF), lambda i: (layer, 0, 0)),
                  weight(D_FF, D_MODEL),
                  pl.BlockSpec((1, D_MODEL), lambda i: (0, 0))],
        out_specs=tile(D_MODEL),
        out_shape=jax.ShapeDtypeStruct((n, D_MODEL), F32),
        scratch_shapes=[pltpu.VMEM(*s) for s in scratch],
        compiler_params=pltpu.CompilerParams(dimension_semantics=("arbitrary",),
                                             vmem_limit_bytes=_vmem_limit(streamed, resident, scratch)),
        name="outproj_ffn",
    )(x, ya, yb, yc, gb, w_out, gf, w_up, cfw, w_dn, fg)


def kernel(x, norm_mix_g, w_in, conv_a_w, conv_c_w, conv_c_b, ln_c_g, ln_c_b, out_norm_g, w_out, norm_ffn_g, w_up, conv_f_w, w_down, rel_bias, final_g):
    batch, seq, d_model = x.shape
    depth = w_in.shape[0]
    assert d_model == D_MODEL and seq % T_IN == 0 and seq % T_FFN == 0 and seq % (BLK * max(DILATIONS)) == 0
    assert w_in.shape[2] == IN_COLS and w_up.shape[2] == 2 * D_FF and D_FF % FF_CHUNK == 0

    rows = lambda a: a.reshape(depth, 1, a.shape[-1])
    w_in_b, w_out_b, w_up_b, w_dn_b = (w.astype(BF16) for w in (w_in, w_out, w_up, w_down))
    g_mix, g_ffn, ccb, lng, lnb = (rows(a) for a in (norm_mix_g, norm_ffn_g, conv_c_b, ln_c_g, ln_c_b))
    g_a = rows(out_norm_g[:, 0:A_WIDTH])
    g_b = rows(out_norm_g[:, A_WIDTH:A_WIDTH + B_WIDTH])
    g_c = rows(out_norm_g[:, A_WIDTH + B_WIDTH:])
    fg = final_g.reshape(1, D_MODEL)

    bias_tab = _bias_table(rel_bias)
    xf = x.reshape(batch * seq, D_MODEL)
    for layer in range(depth):
        ya, qkv, yc = _inproj(xf, layer, seq, g_mix, w_in_b, conv_a_w, conv_c_w, ccb, lng, lnb, g_a, g_c)
        yb = _attention(qkv, bias_tab).reshape(batch * seq, B_WIDTH)
        xf = _ffn(xf, ya, yb, yc, layer, seq, layer == depth - 1, g_b, w_out_b, g_ffn, w_up_b, conv_f_w, w_dn_b, fg)
    return xf.reshape(batch, seq, D_MODEL)
```

```python
import functools
import math

import numpy as np
import jax
import jax.numpy as jnp
from jax import lax
from jax.experimental import pallas as pl
from jax.experimental.pallas import tpu as pltpu

F32 = jnp.float32
BF16 = jnp.bfloat16

D_MODEL = 1024
D_HEAD = 64
A_WIDTH = 256
B_HEADS = 8
B_WIDTH = 512
C_WIDTH = 256
IN_COLS = 3 * A_WIDTH + 3 * B_WIDTH + 2 * C_WIDTH
QKV_LO = 3 * A_WIDTH
C_LO = QKV_LO + 3 * B_WIDTH
DILATIONS = (1, 4, 16)
N_KEYS = 128
BLK = 128
NUM_BUCKETS = 32
MAX_DISTANCE = 2048
SHORT_CONV = 3
CONFORMER_CONV = 31
FFN_CONV = 3
D_FF = 2816
EPS = 1e-6
NEG = -1e30
LOG2E = math.log2(math.e)

FFN_WEIGHTS = ((D_MODEL, D_MODEL), (D_MODEL, 2 * D_FF), (D_FF, D_MODEL))

LANES = 128
SUBLANES = 8
BF16_ROWS = 16
COMPILER_VMEM = 8 * 1024 * 1024

T_IN = 512
T_FFN = 512
CONV_ROWS = 64
FF_CHUNK = 256
A_CARRY = SUBLANES
C_CARRY = 32
BIAS_ROWS = 16


def _nbytes(shape, dtype):
    return math.prod(shape) * jnp.dtype(dtype).itemsize


def _vmem_limit(streamed, resident, scratch):
    total = 2 * sum(_nbytes(*b) for b in streamed) + sum(_nbytes(*b) for b in resident + scratch)
    return total + COMPILER_VMEM


def _rms(x, g):
    return x * lax.rsqrt(jnp.mean(x * x, axis=-1, keepdims=True) + EPS) * g


def _dot(a, b):
    return jnp.dot(a, b, preferred_element_type=F32)


def _bucket_table():
    rel = np.arange(BLK)[:, None] - np.arange(2 * BLK)[None, :] + BLK
    valid = (rel >= 0) & (rel <= N_KEYS)
    max_exact = NUM_BUCKETS // 2
    out = []
    for d in DILATIONS:
        dist = np.maximum(rel, 0) * d
        d_f = np.maximum(dist, 1).astype(np.float32)
        large = max_exact + (np.log(d_f / np.float32(max_exact)) / np.float32(math.log(MAX_DISTANCE / max_exact))
                             * np.float32(NUM_BUCKETS - max_exact)).astype(np.int32)
        large = np.minimum(large, NUM_BUCKETS - 1)
        bucket = np.where(dist < max_exact, dist, large)
        out.append(np.where(valid, bucket, -1).astype(np.int32))
    return np.stack(out)


def _bias_kernel(rb_ref, bucket_ref, o_ref):
    for r0 in range(0, BLK, BIAS_ROWS):
        bucket = bucket_ref[r0:r0 + BIAS_ROWS, :]
        acc = [jnp.full(bucket.shape, NEG, F32) for _ in range(B_HEADS)]
        for b in range(NUM_BUCKETS):
            hit = bucket == b
            acc = [jnp.where(hit, rb_ref[b, h] * LOG2E, acc[h]) for h in range(B_HEADS)]
        for h in range(B_HEADS):
            o_ref[h, r0:r0 + BIAS_ROWS, :] = acc[h]


def _bias_table(rel_bias):
    buckets = jnp.asarray(_bucket_table())
    return pl.pallas_call(
        _bias_kernel,
        grid=(len(DILATIONS),),
        in_specs=[pl.BlockSpec(memory_space=pltpu.SMEM),
                  pl.BlockSpec((None, BLK, 2 * BLK), lambda br: (br, 0, 0))],
        out_specs=pl.BlockSpec((None, B_HEADS, BLK, 2 * BLK), lambda br: (br, 0, 0, 0)),
        out_shape=jax.ShapeDtypeStruct((len(DILATIONS), B_HEADS, BLK, 2 * BLK), F32),
        name="bias_table",
    )(rel_bias, buckets)


def _inproj_kernel(x_ref, g_ref, w_ref, caw_ref, ccw_ref, ccb_ref, lng_ref, lnb_ref, ga_ref, gc_ref, *refs,
                   tiles_per_seq):
    n_w = len(FFN_WEIGHTS)
    w_f32, (ya_ref, qkv_ref, yc_ref), w_bf16 = refs[:n_w], refs[n_w:n_w + 3], refs[n_w + 3:2 * n_w + 3]
    tbuf, ubuf, ushift = refs[2 * n_w + 3:]
    t_rows = x_ref.shape[0]

    for src, dst in zip(w_f32, w_bf16):
        dst[...] = src[...].astype(BF16)

    @pl.when(pl.program_id(0) % tiles_per_seq == 0)
    def _():
        tbuf[0:A_CARRY, :] = jnp.zeros((A_CARRY, A_WIDTH), F32)
        ubuf[0:C_CARRY, :] = jnp.zeros((C_CARRY, C_WIDTH), F32)

    h = _rms(x_ref[...], g_ref[...]).astype(BF16)

    zc = _dot(h, w_ref[:, C_LO:IN_COLS])
    ubuf[C_CARRY:C_CARRY + t_rows, :] = zc[:, 0:C_WIDTH] * jax.nn.sigmoid(zc[:, C_WIDTH:2 * C_WIDTH])
    ccw = ccw_ref[...]
    first = C_CARRY - (CONFORMER_CONV - 1)
    shift_rows = C_CARRY + t_rows - SUBLANES
    for ph in range(1, SUBLANES):
        ushift[ph - 1, :, :] = ubuf[pl.ds(ph, shift_rows), :]
    for rc in range(t_rows // CONV_ROWS):
        acc = jnp.broadcast_to(ccb_ref[...], (CONV_ROWS, C_WIDTH))
        for k in range(CONFORMER_CONV):
            tiles, ph = divmod(first + k, SUBLANES)
            rows = pl.ds(rc * CONV_ROWS + tiles * SUBLANES, CONV_ROWS)
            acc = acc + ccw[k:k + 1] * (ubuf[rows, :] if ph == 0 else ushift[ph - 1, rows, :])
        mu = jnp.mean(acc, axis=-1, keepdims=True)
        cen = acc - mu
        var = jnp.mean(cen * cen, axis=-1, keepdims=True)
        ln = cen * lax.rsqrt(var + EPS) * lng_ref[...] + lnb_ref[...]
        yc = ln * jax.nn.sigmoid(ln)
        yc_ref[rc * CONV_ROWS:(rc + 1) * CONV_ROWS, :] = _rms(yc, gc_ref[...]).astype(BF16)
    ubuf[0:C_CARRY, :] = ubuf[t_rows:t_rows + C_CARRY, :]

    za = _dot(h, w_ref[:, 0:QKV_LO])
    a_h, a_b, a_c = za[:, 0:A_WIDTH], za[:, A_WIDTH:2 * A_WIDTH], za[:, 2 * A_WIDTH:3 * A_WIDTH]
    t = a_c * a_h
    tbuf[A_CARRY:A_CARRY + t_rows, :] = t
    caw = caw_ref[...]
    conv = (caw[0:1] * tbuf[A_CARRY - 2:A_CARRY - 2 + t_rows, :]
            + caw[1:2] * tbuf[A_CARRY - 1:A_CARRY - 1 + t_rows, :]
            + caw[2:3] * t)
    tbuf[0:A_CARRY, :] = tbuf[t_rows:t_rows + A_CARRY, :]
    ya_ref[...] = _rms(a_b * conv, ga_ref[...]).astype(BF16)

    zqkv = _dot(h, w_ref[:, QKV_LO:C_LO])
    pairs = B_WIDTH // LANES
    for g in range(3 * pairs):
        slab = zqkv[:, g * LANES:(g + 1) * LANES]
        qkv_ref[g, :, :] = slab * (D_HEAD ** -0.5 * LOG2E) if g < pairs else slab


def _inproj(x, layer, seq, g, w_in, caw, ccw, ccb, lng, lnb, ga, gc, ffn_weights):
    n = x.shape[0]
    n_steps = n // T_IN
    tiles_per_seq = seq // T_IN
    slabs = 3 * B_WIDTH // LANES
    row = lambda width: pl.BlockSpec((None, 1, width), lambda i: (layer, 0, 0))
    scratch = [((A_CARRY + T_IN, A_WIDTH), F32), ((C_CARRY + T_IN, C_WIDTH), F32),
               ((SUBLANES - 1, C_CARRY + T_IN - SUBLANES, C_WIDTH), F32)]
    streamed = [((T_IN, D_MODEL), F32), ((T_IN, A_WIDTH), BF16), ((slabs, T_IN, LANES), F32), ((T_IN, C_WIDTH), BF16)]
    w_in_specs, w_out_specs, w_shapes = [], [], []
    for w, (rows, cols) in zip(ffn_weights, FFN_WEIGHTS):
        assert w.shape[1:] == (rows, cols) and rows % BF16_ROWS == 0
        pieces = math.gcd(n_steps, rows // BF16_ROWS)
        piece, repeat = rows // pieces, n_steps // pieces
        w_in_specs.append(pl.BlockSpec((None, piece, cols), lambda i, repeat=repeat: (layer, i // repeat, 0)))
        w_out_specs.append(pl.BlockSpec((piece, cols), lambda i, repeat=repeat: (i // repeat, 0)))
        w_shapes.append(jax.ShapeDtypeStruct((rows, cols), BF16))
        streamed += [((piece, cols), F32), ((piece, cols), BF16)]
    return pl.pallas_call(
        functools.partial(_inproj_kernel, tiles_per_seq=tiles_per_seq),
        grid=(n_steps,),
        in_specs=[pl.BlockSpec((T_IN, D_MODEL), lambda i: (i, 0)),
                  row(D_MODEL),
                  pl.BlockSpec((None, D_MODEL, IN_COLS), lambda i: (layer, 0, 0), pipeline_mode=pl.Buffered(1)),
                  pl.BlockSpec((None, SHORT_CONV, A_WIDTH), lambda i: (layer, 0, 0)),
                  pl.BlockSpec((None, CONFORMER_CONV, C_WIDTH), lambda i: (layer, 0, 0)),
                  row(C_WIDTH), row(C_WIDTH), row(C_WIDTH), row(A_WIDTH), row(C_WIDTH)] + w_in_specs,
        out_specs=[pl.BlockSpec((T_IN, A_WIDTH), lambda i: (i, 0)),
                   pl.BlockSpec((None, slabs, T_IN, LANES),
                                lambda i: (i // tiles_per_seq, 0, i % tiles_per_seq, 0)),
                   pl.BlockSpec((T_IN, C_WIDTH), lambda i: (i, 0))] + w_out_specs,
        out_shape=[jax.ShapeDtypeStruct((n, A_WIDTH), BF16),
                   jax.ShapeDtypeStruct((n // seq, slabs, seq, LANES), F32),
                   jax.ShapeDtypeStruct((n, C_WIDTH), BF16)] + w_shapes,
        scratch_shapes=[pltpu.VMEM(*s) for s in scratch],
        compiler_params=pltpu.CompilerParams(
            dimension_semantics=("arbitrary",),
            vmem_limit_bytes=_vmem_limit(streamed, [((D_MODEL, IN_COLS), BF16)], scratch)),
        name="inproj_mix",
    )(x, g, w_in, caw, ccw, ccb, lng, lnb, ga, gc, *ffn_weights)


def _attn_kernel(q_ref, k_ref, v_ref, bias_ref, o_ref, g4q, g4k, g4v, qs, ks, v0s, v1s, *stats):
    seq = q_ref.shape[0]
    n_blocks = seq // BLK
    quarter = seq // 4
    low = lax.broadcasted_iota(jnp.int32, (1, LANES), 1) < D_HEAD
    nt = (((1,), (1,)), ((), ()))
    head_lanes = (low, jnp.logical_not(low))
    v_ones = (v0s, v1s)

    def fill(dst, qv, kv, vv):
        qs[dst, :] = qv.astype(BF16)
        ks[dst, :] = kv.astype(BF16)
        v0s[dst, :] = jnp.where(low, vv, 1.0).astype(BF16)
        v1s[dst, :] = jnp.where(low, 1.0, vv).astype(BF16)

    for br, d in enumerate(DILATIONS):
        if d == 1:
            for c in range(4):
                rows = pl.ds(c * quarter, quarter)
                fill(rows, q_ref[rows, :], k_ref[rows, :], v_ref[rows, :])
        elif d == 4:
            for r in range(4):
                src, dst = pl.ds(r, quarter, stride=4), pl.ds(r * quarter, quarter)
                qv, kv, vv = q_ref[src, :], k_ref[src, :], v_ref[src, :]
                g4q[dst, :], g4k[dst, :], g4v[dst, :] = qv, kv, vv
                fill(dst, qv, kv, vv)
        else:
            for r in range(16):
                src, dst = pl.ds((r % 4) * quarter + r // 4, BLK, stride=4), pl.ds(r * BLK, BLK)
                fill(dst, g4q[src, :], g4k[src, :], g4v[src, :])

        per_class = seq // d // BLK
        blocks = []
        for b in range(n_blocks):
            r, j = divmod(b, per_class)
            rows = pl.ds(b * BLK, BLK)
            keys = rows if j == 0 else pl.ds((b - 1) * BLK, 2 * BLK)
            cols = pl.ds(BLK, BLK) if j == 0 else pl.ds(0, 2 * BLK)
            out = pl.ds((r % 4) * quarter + r // 4, BLK, stride=4) if d == 16 else rows
            blocks.append((rows, keys, cols, out))

        scores = []
        for rows, keys, cols, out in blocks:
            for hh in range(2):
                q_head = jnp.where(head_lanes[hh], qs[rows, :], jnp.zeros((), BF16))
                scores.append(lax.dot_general(q_head, ks[keys, :], nt, preferred_element_type=F32)
                              + bias_ref[br, hh, :, cols])
        probs = []
        for i, (rows, keys, cols, out) in enumerate(blocks):
            for hh in range(2):
                s = scores[2 * i + hh]
                m = jnp.max(s, axis=-1, keepdims=True)
                probs.append(jnp.exp2(s - m).astype(BF16))
                stats[2 * (2 * br + hh) + 1][out, :] = jnp.broadcast_to(m, (BLK, LANES))
        for i, (rows, keys, cols, out) in enumerate(blocks):
            for hh in range(2):
                stats[2 * (2 * br + hh)][out, :] = _dot(probs[2 * i + hh], v_ones[hh][keys, :])

    for t in range(n_blocks):
        r4, n = divmod(t, 4)
        rows4 = pl.ds(t * BLK, BLK)
        nat = pl.ds(4 * BLK * n + r4, BLK, stride=4)
        y = []
        for hh in range(2):
            where = (nat, rows4, rows4)
            ms = [stats[2 * (2 * br + hh) + 1][where[br], :] for br in range(len(DILATIONS))]
            m_all = jnp.maximum(jnp.maximum(ms[0], ms[1]), ms[2])
            tot = jnp.zeros((BLK, LANES), F32)
            for br in range(len(DILATIONS)):
                tot = tot + jnp.exp2(ms[br] - m_all) * stats[2 * (2 * br + hh)][where[br], :]
            y.append(tot / pltpu.roll(tot, D_HEAD, axis=1))
        o_ref[nat, :] = jnp.where(low, y[0], y[1])


def _attention(qkv, bias_tab):
    batch, _, seq, _ = qkv.shape
    pairs = B_WIDTH // LANES
    n_br = len(DILATIONS)
    blk = lambda off: pl.BlockSpec((None, None, seq, LANES), lambda g, b: (b, off * pairs + g, 0, 0))
    scratch = [((seq, LANES), F32)] * 3 + [((seq, LANES), BF16)] * 4 + [((seq, LANES), F32)] * (4 * n_br)
    streamed = [((seq, LANES), F32)] * 4 + [((n_br, 2, BLK, 2 * BLK), F32)]
    return pl.pallas_call(
        _attn_kernel,
        grid=(pairs, batch),
        in_specs=[blk(0), blk(1), blk(2),
                  pl.BlockSpec((n_br, 2, BLK, 2 * BLK), lambda g, b: (0, g, 0, 0))],
        out_specs=pl.BlockSpec((None, seq, LANES), lambda g, b: (b, 0, g)),
        out_shape=jax.ShapeDtypeStruct((batch, seq, B_WIDTH), F32),
        scratch_shapes=[pltpu.VMEM(*s) for s in scratch],
        compiler_params=pltpu.CompilerParams(dimension_semantics=("arbitrary", "arbitrary"),
                                             vmem_limit_bytes=_vmem_limit(streamed, [], scratch)),
        name="dilated_attn",
    )(qkv, qkv, qkv, bias_tab)


def _ffn_kernel(x_ref, ya_ref, yb_ref, yc_ref, gb_ref, wout_ref, gf_ref, wup_ref, cfw_ref, wdn_ref, fg_ref,
                o_ref, carry, act, *, tiles_per_seq, final):
    t_rows = x_ref.shape[0]

    @pl.when(pl.program_id(0) % tiles_per_seq == 0)
    def _():
        carry[...] = jnp.zeros((A_CARRY, 2 * D_FF), F32)

    y = jnp.concatenate([ya_ref[...], _rms(yb_ref[...], gb_ref[...]).astype(BF16), yc_ref[...]], axis=-1)
    x1 = x_ref[...] + _dot(y, wout_ref[...])
    h = _rms(x1, gf_ref[...]).astype(BF16)

    def conv3(c0):
        cols = pl.ds(c0, FF_CHUNK)
        up = _dot(h, wup_ref[:, cols])
        ext = jnp.concatenate([carry[:, cols], up], axis=0)
        carry[:, cols] = up[t_rows - A_CARRY:, :]
        w = cfw_ref[:, cols]
        return (w[0:1] * pltpu.roll(ext, 2, axis=0)[A_CARRY:, :]
                + w[1:2] * pltpu.roll(ext, 1, axis=0)[A_CARRY:, :]
                + w[2:3] * up)

    for c in range(D_FF // FF_CHUNK):
        gate = conv3(c * FF_CHUNK)
        val = conv3(D_FF + c * FF_CHUNK)
        act[:, c * FF_CHUNK:(c + 1) * FF_CHUNK] = (gate * jax.nn.sigmoid(gate) * val).astype(BF16)

    x2 = x1 + _dot(act[...], wdn_ref[...])
    if final:
        x2 = _rms(x2, fg_ref[...])
    o_ref[...] = x2


def _ffn(x, ya, yb, yc, layer, seq, final, gb, w_out, gf, w_up, cfw, w_dn, fg):
    n = x.shape[0]
    tile = lambda width: pl.BlockSpec((T_FFN, width), lambda i: (i, 0))
    row = lambda width: pl.BlockSpec((None, 1, width), lambda i: (layer, 0, 0))
    weight = lambda r, c: pl.BlockSpec((r, c), lambda i: (0, 0), pipeline_mode=pl.Buffered(1))
    scratch = [((A_CARRY, 2 * D_FF), F32), ((T_FFN, D_FF), BF16)]
    streamed = [((T_FFN, D_MODEL), F32)] * 2 + [((T_FFN, A_WIDTH), BF16), ((T_FFN, B_WIDTH), F32),
                                                ((T_FFN, C_WIDTH), BF16)]
    resident = [((D_MODEL, D_MODEL), BF16), ((D_MODEL, 2 * D_FF), BF16), ((D_FF, D_MODEL), BF16)]
    return pl.pallas_call(
        functools.partial(_ffn_kernel, tiles_per_seq=seq // T_FFN, final=final),
        grid=(n // T_FFN,),
        in_specs=[tile(D_MODEL), tile(A_WIDTH), tile(B_WIDTH), tile(C_WIDTH),
                  row(B_WIDTH), weight(D_MODEL, D_MODEL), row(D_MODEL), weight(D_MODEL, 2 * D_FF),
                  pl.BlockSpec((None, FFN_CONV, 2 * D_FF), lambda i: (layer, 0, 0)),
                  weight(D_FF, D_MODEL),
                  pl.BlockSpec((1, D_MODEL), lambda i: (0, 0))],
        out_specs=tile(D_MODEL),
        out_shape=jax.ShapeDtypeStruct((n, D_MODEL), F32),
        scratch_shapes=[pltpu.VMEM(*s) for s in scratch],
        compiler_params=pltpu.CompilerParams(dimension_semantics=("arbitrary",),
                                             vmem_limit_bytes=_vmem_limit(streamed, resident, scratch)),
        name="outproj_ffn",
    )(x, ya, yb, yc, gb, w_out, gf, w_up, cfw, w_dn, fg)


def kernel(x, norm_mix_g, w_in, conv_a_w, conv_c_w, conv_c_b, ln_c_g, ln_c_b, out_norm_g, w_out, norm_ffn_g, w_up, conv_f_w, w_down, rel_bias, final_g):
    batch, seq, d_model = x.shape
    depth = w_in.shape[0]
    assert d_model == D_MODEL and seq % T_IN == 0 and seq % T_FFN == 0 and seq % (BLK * max(DILATIONS)) == 0
    assert w_in.shape[2] == IN_COLS and w_up.shape[2] == 2 * D_FF and D_FF % FF_CHUNK == 0

    rows = lambda a: a.reshape(depth, 1, a.shape[-1])
    w_in_b = w_in.astype(BF16)
    g_mix, g_ffn, ccb, lng, lnb = (rows(a) for a in (norm_mix_g, norm_ffn_g, conv_c_b, ln_c_g, ln_c_b))
    g_a = rows(out_norm_g[:, 0:A_WIDTH])
    g_b = rows(out_norm_g[:, A_WIDTH:A_WIDTH + B_WIDTH])
    g_c = rows(out_norm_g[:, A_WIDTH + B_WIDTH:])
    fg = final_g.reshape(1, D_MODEL)

    bias_tab = _bias_table(rel_bias)
    xf = x.reshape(batch * seq, D_MODEL)
    for layer in range(depth):
        ya, qkv, yc, w_out_b, w_up_b, w_dn_b = _inproj(xf, layer, seq, g_mix, w_in_b, conv_a_w, conv_c_w, ccb, lng, lnb,
                                                       g_a, g_c, (w_out, w_up, w_down))
        yb = _attention(qkv, bias_tab).reshape(batch * seq, B_WIDTH)
        xf = _ffn(xf, ya, yb, yc, layer, seq, layer == depth - 1, g_b, w_out_b, g_ffn, w_up_b, conv_f_w, w_dn_b, fg)
    return xf.reshape(batch, seq, D_MODEL)
```

```python
import functools
import math

import numpy as np
import jax
import jax.numpy as jnp
from jax import lax
from jax.experimental import pallas as pl
from jax.experimental.pallas import tpu as pltpu

F32 = jnp.float32
BF16 = jnp.bfloat16

D_MODEL = 1024
D_HEAD = 64
A_WIDTH = 256
B_HEADS = 8
B_WIDTH = 512
C_WIDTH = 256
IN_COLS = 3 * A_WIDTH + 3 * B_WIDTH + 2 * C_WIDTH
QKV_LO = 3 * A_WIDTH
C_LO = QKV_LO + 3 * B_WIDTH
DILATIONS = (1, 4, 16)
N_KEYS = 128
BLK = 128
NUM_BUCKETS = 32
MAX_DISTANCE = 2048
SHORT_CONV = 3
CONFORMER_CONV = 31
FFN_CONV = 3
D_FF = 2816
EPS = 1e-6
NEG = -1e30
LOG2E = math.log2(math.e)

FFN_WEIGHTS = ((D_MODEL, D_MODEL), (D_MODEL, 2 * D_FF), (D_FF, D_MODEL))

LANES = 128
SUBLANES = 8
BF16_ROWS = 16
COMPILER_VMEM = 8 * 1024 * 1024

T_IN = 512
T_FFN = 1024
CONV_ROWS = 64
FF_CHUNK = 256
A_CARRY = SUBLANES
C_CARRY = 32
BIAS_ROWS = 16


def _nbytes(shape, dtype):
    return math.prod(shape) * jnp.dtype(dtype).itemsize


def _cast_slices(w, layer, n_steps):
    rows, cols = w.shape[1:]
    assert rows % BF16_ROWS == 0
    pieces = math.gcd(n_steps, rows // BF16_ROWS)
    piece, repeat = rows // pieces, n_steps // pieces
    src = pl.BlockSpec((None, piece, cols), lambda i: (layer, i // repeat, 0))
    dst = pl.BlockSpec((piece, cols), lambda i: (i // repeat, 0))
    return src, dst, jax.ShapeDtypeStruct((rows, cols), BF16), [((piece, cols), F32), ((piece, cols), BF16)]


def _vmem_limit(streamed, resident, scratch):
    total = 2 * sum(_nbytes(*b) for b in streamed) + sum(_nbytes(*b) for b in resident + scratch)
    return total + COMPILER_VMEM


def _rms(x, g):
    return x * lax.rsqrt(jnp.mean(x * x, axis=-1, keepdims=True) + EPS) * g


def _dot(a, b):
    return jnp.dot(a, b, preferred_element_type=F32)


def _bucket_table():
    rel = np.arange(BLK)[:, None] - np.arange(2 * BLK)[None, :] + BLK
    valid = (rel >= 0) & (rel <= N_KEYS)
    max_exact = NUM_BUCKETS // 2
    out = []
    for d in DILATIONS:
        dist = np.maximum(rel, 0) * d
        d_f = np.maximum(dist, 1).astype(np.float32)
        large = max_exact + (np.log(d_f / np.float32(max_exact)) / np.float32(math.log(MAX_DISTANCE / max_exact))
                             * np.float32(NUM_BUCKETS - max_exact)).astype(np.int32)
        large = np.minimum(large, NUM_BUCKETS - 1)
        bucket = np.where(dist < max_exact, dist, large)
        out.append(np.where(valid, bucket, -1).astype(np.int32))
    return np.stack(out)


def _bias_kernel(rb_ref, bucket_ref, o_ref):
    for r0 in range(0, BLK, BIAS_ROWS):
        bucket = bucket_ref[r0:r0 + BIAS_ROWS, :]
        acc = [jnp.full(bucket.shape, NEG, F32) for _ in range(B_HEADS)]
        for b in range(NUM_BUCKETS):
            hit = bucket == b
            acc = [jnp.where(hit, rb_ref[b, h] * LOG2E, acc[h]) for h in range(B_HEADS)]
        for h in range(B_HEADS):
            o_ref[h, r0:r0 + BIAS_ROWS, :] = acc[h]


def _bias_table(rel_bias):
    buckets = jnp.asarray(_bucket_table())
    return pl.pallas_call(
        _bias_kernel,
        grid=(len(DILATIONS),),
        in_specs=[pl.BlockSpec(memory_space=pltpu.SMEM),
                  pl.BlockSpec((None, BLK, 2 * BLK), lambda br: (br, 0, 0))],
        out_specs=pl.BlockSpec((None, B_HEADS, BLK, 2 * BLK), lambda br: (br, 0, 0, 0)),
        out_shape=jax.ShapeDtypeStruct((len(DILATIONS), B_HEADS, BLK, 2 * BLK), F32),
        name="bias_table",
    )(rel_bias, buckets)


def _inproj_kernel(x_ref, gmix_ref, w_ref, caw_ref, ccw_ref, ccb_ref, lng_ref, lnb_ref, gout_ref, *refs,
                   layer, tiles_per_seq):
    one = lambda ref, lo=0, hi=None: ref[layer:layer + 1, lo:hi]
    g_mix, ccb, lng, lnb = one(gmix_ref), one(ccb_ref), one(lng_ref), one(lnb_ref)
    ga, gc = one(gout_ref, 0, A_WIDTH), one(gout_ref, A_WIDTH + B_WIDTH)
    n_w = len(FFN_WEIGHTS)
    w_f32, (ya_ref, qkv_ref, yc_ref), w_bf16 = refs[:n_w], refs[n_w:n_w + 3], refs[n_w + 3:2 * n_w + 3]
    tbuf, ubuf, ushift = refs[2 * n_w + 3:]
    t_rows = x_ref.shape[0]

    for src, dst in zip(w_f32, w_bf16):
        dst[...] = src[...].astype(BF16)

    @pl.when(pl.program_id(0) % tiles_per_seq == 0)
    def _():
        tbuf[0:A_CARRY, :] = jnp.zeros((A_CARRY, A_WIDTH), F32)
        ubuf[0:C_CARRY, :] = jnp.zeros((C_CARRY, C_WIDTH), F32)

    h = _rms(x_ref[...], g_mix).astype(BF16)

    zc = _dot(h, w_ref[:, C_LO:IN_COLS])
    ubuf[C_CARRY:C_CARRY + t_rows, :] = zc[:, 0:C_WIDTH] * jax.nn.sigmoid(zc[:, C_WIDTH:2 * C_WIDTH])
    ccw = ccw_ref[...]
    first = C_CARRY - (CONFORMER_CONV - 1)
    shift_rows = C_CARRY + t_rows - SUBLANES
    for ph in range(1, SUBLANES):
        ushift[ph - 1, :, :] = ubuf[pl.ds(ph, shift_rows), :]
    for rc in range(t_rows // CONV_ROWS):
        acc = jnp.broadcast_to(ccb, (CONV_ROWS, C_WIDTH))
        for k in range(CONFORMER_CONV):
            tiles, ph = divmod(first + k, SUBLANES)
            rows = pl.ds(rc * CONV_ROWS + tiles * SUBLANES, CONV_ROWS)
            acc = acc + ccw[k:k + 1] * (ubuf[rows, :] if ph == 0 else ushift[ph - 1, rows, :])
        mu = jnp.mean(acc, axis=-1, keepdims=True)
        cen = acc - mu
        var = jnp.mean(cen * cen, axis=-1, keepdims=True)
        ln = cen * lax.rsqrt(var + EPS) * lng + lnb
        yc = ln * jax.nn.sigmoid(ln)
        yc_ref[rc * CONV_ROWS:(rc + 1) * CONV_ROWS, :] = _rms(yc, gc).astype(BF16)
    ubuf[0:C_CARRY, :] = ubuf[t_rows:t_rows + C_CARRY, :]

    za = _dot(h, w_ref[:, 0:QKV_LO])
    a_h, a_b, a_c = za[:, 0:A_WIDTH], za[:, A_WIDTH:2 * A_WIDTH], za[:, 2 * A_WIDTH:3 * A_WIDTH]
    t = a_c * a_h
    tbuf[A_CARRY:A_CARRY + t_rows, :] = t
    caw = caw_ref[...]
    conv = (caw[0:1] * tbuf[A_CARRY - 2:A_CARRY - 2 + t_rows, :]
            + caw[1:2] * tbuf[A_CARRY - 1:A_CARRY - 1 + t_rows, :]
            + caw[2:3] * t)
    tbuf[0:A_CARRY, :] = tbuf[t_rows:t_rows + A_CARRY, :]
    ya_ref[...] = _rms(a_b * conv, ga).astype(BF16)

    zqkv = _dot(h, w_ref[:, QKV_LO:C_LO])
    pairs = B_WIDTH // LANES
    for g in range(3 * pairs):
        slab = zqkv[:, g * LANES:(g + 1) * LANES]
        qkv_ref[g, :, :] = slab * (D_HEAD ** -0.5 * LOG2E) if g < pairs else slab


def _inproj(x, layer, seq, g_mix, w_in, caw, ccw, ccb, lng, lnb, g_out, ffn_weights):
    n = x.shape[0]
    n_steps = n // T_IN
    tiles_per_seq = seq // T_IN
    slabs = 3 * B_WIDTH // LANES
    whole = lambda a: pl.BlockSpec(a.shape, lambda i: (0, 0))
    scratch = [((A_CARRY + T_IN, A_WIDTH), F32), ((C_CARRY + T_IN, C_WIDTH), F32),
               ((SUBLANES - 1, C_CARRY + T_IN - SUBLANES, C_WIDTH), F32)]
    streamed = [((T_IN, D_MODEL), F32), ((T_IN, A_WIDTH), BF16), ((slabs, T_IN, LANES), F32), ((T_IN, C_WIDTH), BF16)]
    w_in_specs, w_out_specs, w_shapes = [], [], []
    for w, shape in zip(ffn_weights, FFN_WEIGHTS):
        assert w.shape[1:] == shape
        src, dst, out, blocks = _cast_slices(w, layer, n_steps)
        w_in_specs.append(src)
        w_out_specs.append(dst)
        w_shapes.append(out)
        streamed += blocks
    return pl.pallas_call(
        functools.partial(_inproj_kernel, layer=layer, tiles_per_seq=tiles_per_seq),
        grid=(n_steps,),
        in_specs=[pl.BlockSpec((T_IN, D_MODEL), lambda i: (i, 0)),
                  whole(g_mix),
                  pl.BlockSpec((D_MODEL, IN_COLS), lambda i: (0, 0), pipeline_mode=pl.Buffered(1)),
                  pl.BlockSpec((None, SHORT_CONV, A_WIDTH), lambda i: (layer, 0, 0)),
                  pl.BlockSpec((None, CONFORMER_CONV, C_WIDTH), lambda i: (layer, 0, 0)),
                  whole(ccb), whole(lng), whole(lnb), whole(g_out)] + w_in_specs,
        out_specs=[pl.BlockSpec((T_IN, A_WIDTH), lambda i: (i, 0)),
                   pl.BlockSpec((None, slabs, T_IN, LANES),
                                lambda i: (i // tiles_per_seq, 0, i % tiles_per_seq, 0)),
                   pl.BlockSpec((T_IN, C_WIDTH), lambda i: (i, 0))] + w_out_specs,
        out_shape=[jax.ShapeDtypeStruct((n, A_WIDTH), BF16),
                   jax.ShapeDtypeStruct((n // seq, slabs, seq, LANES), F32),
                   jax.ShapeDtypeStruct((n, C_WIDTH), BF16)] + w_shapes,
        scratch_shapes=[pltpu.VMEM(*s) for s in scratch],
        compiler_params=pltpu.CompilerParams(
            dimension_semantics=("arbitrary",),
            vmem_limit_bytes=_vmem_limit(streamed, [((D_MODEL, IN_COLS), BF16)], scratch)),
        name="inproj_mix",
    )(x, g_mix, w_in, caw, ccw, ccb, lng, lnb, g_out, *ffn_weights)


def _attn_kernel(q_ref, k_ref, v_ref, bias_ref, o_ref, g4q, g4k, g4v, qs, ks, v0s, v1s, *stats):
    seq = q_ref.shape[0]
    n_blocks = seq // BLK
    quarter = seq // 4
    low = lax.broadcasted_iota(jnp.int32, (1, LANES), 1) < D_HEAD
    nt = (((1,), (1,)), ((), ()))
    head_lanes = (low, jnp.logical_not(low))
    v_ones = (v0s, v1s)

    def fill(dst, qv, kv, vv):
        qs[dst, :] = qv.astype(BF16)
        ks[dst, :] = kv.astype(BF16)
        v0s[dst, :] = jnp.where(low, vv, 1.0).astype(BF16)
        v1s[dst, :] = jnp.where(low, 1.0, vv).astype(BF16)

    for br, d in enumerate(DILATIONS):
        if d == 1:
            for c in range(4):
                rows = pl.ds(c * quarter, quarter)
                fill(rows, q_ref[rows, :], k_ref[rows, :], v_ref[rows, :])
        elif d == 4:
            for r in range(4):
                src, dst = pl.ds(r, quarter, stride=4), pl.ds(r * quarter, quarter)
                qv, kv, vv = q_ref[src, :], k_ref[src, :], v_ref[src, :]
                g4q[dst, :], g4k[dst, :], g4v[dst, :] = qv, kv, vv
                fill(dst, qv, kv, vv)
        else:
            for r in range(16):
                src, dst = pl.ds((r % 4) * quarter + r // 4, BLK, stride=4), pl.ds(r * BLK, BLK)
                fill(dst, g4q[src, :], g4k[src, :], g4v[src, :])

        per_class = seq // d // BLK
        blocks = []
        for b in range(n_blocks):
            r, j = divmod(b, per_class)
            rows = pl.ds(b * BLK, BLK)
            keys = rows if j == 0 else pl.ds((b - 1) * BLK, 2 * BLK)
            cols = pl.ds(BLK, BLK) if j == 0 else pl.ds(0, 2 * BLK)
            out = pl.ds((r % 4) * quarter + r // 4, BLK, stride=4) if d == 16 else rows
            blocks.append((rows, keys, cols, out))

        scores = []
        for rows, keys, cols, out in blocks:
            for hh in range(2):
                q_head = jnp.where(head_lanes[hh], qs[rows, :], jnp.zeros((), BF16))
                scores.append(lax.dot_general(q_head, ks[keys, :], nt, preferred_element_type=F32)
                              + bias_ref[br, hh, :, cols])
        probs = []
        for i, (rows, keys, cols, out) in enumerate(blocks):
            for hh in range(2):
                s = scores[2 * i + hh]
                m = jnp.max(s, axis=-1, keepdims=True)
                probs.append(jnp.exp2(s - m).astype(BF16))
                stats[2 * (2 * br + hh) + 1][out, :] = jnp.broadcast_to(m, (BLK, LANES))
        for i, (rows, keys, cols, out) in enumerate(blocks):
            for hh in range(2):
                stats[2 * (2 * br + hh)][out, :] = _dot(probs[2 * i + hh], v_ones[hh][keys, :])

    for t in range(n_blocks):
        r4, n = divmod(t, 4)
        rows4 = pl.ds(t * BLK, BLK)
        nat = pl.ds(4 * BLK * n + r4, BLK, stride=4)
        y = []
        for hh in range(2):
            where = (nat, rows4, rows4)
            ms = [stats[2 * (2 * br + hh) + 1][where[br], :] for br in range(len(DILATIONS))]
            m_all = jnp.maximum(jnp.maximum(ms[0], ms[1]), ms[2])
            tot = jnp.zeros((BLK, LANES), F32)
            for br in range(len(DILATIONS)):
                tot = tot + jnp.exp2(ms[br] - m_all) * stats[2 * (2 * br + hh)][where[br], :]
            y.append(tot / pltpu.roll(tot, D_HEAD, axis=1))
        o_ref[nat, :] = jnp.where(low, y[0], y[1])


def _attention(qkv, bias_tab):
    batch, _, seq, _ = qkv.shape
    pairs = B_WIDTH // LANES
    n_br = len(DILATIONS)
    blk = lambda off: pl.BlockSpec((None, None, seq, LANES), lambda g, b: (b, off * pairs + g, 0, 0))
    scratch = [((seq, LANES), F32)] * 3 + [((seq, LANES), BF16)] * 4 + [((seq, LANES), F32)] * (4 * n_br)
    streamed = [((seq, LANES), F32)] * 4 + [((n_br, 2, BLK, 2 * BLK), F32)]
    return pl.pallas_call(
        _attn_kernel,
        grid=(pairs, batch),
        in_specs=[blk(0), blk(1), blk(2),
                  pl.BlockSpec((n_br, 2, BLK, 2 * BLK), lambda g, b: (0, g, 0, 0))],
        out_specs=pl.BlockSpec((None, seq, LANES), lambda g, b: (b, 0, g)),
        out_shape=jax.ShapeDtypeStruct((batch, seq, B_WIDTH), F32),
        scratch_shapes=[pltpu.VMEM(*s) for s in scratch],
        compiler_params=pltpu.CompilerParams(dimension_semantics=("arbitrary", "arbitrary"),
                                             vmem_limit_bytes=_vmem_limit(streamed, [], scratch)),
        name="dilated_attn",
    )(qkv, qkv, qkv, bias_tab)


def _ffn_kernel(x_ref, ya_ref, yb_ref, yc_ref, gout_ref, wout_ref, gffn_ref, wup_ref, cfw_ref, wdn_ref, fg_ref,
                *refs, layer, tiles_per_seq, final):
    g_b = gout_ref[layer:layer + 1, A_WIDTH:A_WIDTH + B_WIDTH]
    g_ffn = gffn_ref[layer:layer + 1, :]
    if final:
        o_ref, carry, act = refs
    else:
        w_next_f32, o_ref, w_next_bf16, carry, act = refs
        w_next_bf16[...] = w_next_f32[...].astype(BF16)
    t_rows = x_ref.shape[0]

    @pl.when(pl.program_id(0) % tiles_per_seq == 0)
    def _():
        carry[...] = jnp.zeros((A_CARRY, 2 * D_FF), F32)

    y = jnp.concatenate([ya_ref[...], _rms(yb_ref[...], g_b).astype(BF16), yc_ref[...]], axis=-1)
    x1 = x_ref[...] + _dot(y, wout_ref[...])
    h = _rms(x1, g_ffn).astype(BF16)

    def conv3(c0):
        cols = pl.ds(c0, FF_CHUNK)
        up = _dot(h, wup_ref[:, cols])
        ext = jnp.concatenate([carry[:, cols], up], axis=0)
        carry[:, cols] = up[t_rows - A_CARRY:, :]
        w = cfw_ref[:, cols]
        return (w[0:1] * pltpu.roll(ext, 2, axis=0)[A_CARRY:, :]
                + w[1:2] * pltpu.roll(ext, 1, axis=0)[A_CARRY:, :]
                + w[2:3] * up)

    for c in range(D_FF // FF_CHUNK):
        gate = conv3(c * FF_CHUNK)
        val = conv3(D_FF + c * FF_CHUNK)
        act[:, c * FF_CHUNK:(c + 1) * FF_CHUNK] = (gate * jax.nn.sigmoid(gate) * val).astype(BF16)

    x2 = x1 + _dot(act[...], wdn_ref[...])
    if final:
        x2 = _rms(x2, fg_ref[...])
    o_ref[...] = x2


def _ffn(x, ya, yb, yc, layer, seq, final, g_out, w_out, g_ffn, w_up, cfw, w_dn, fg, w_in):
    n = x.shape[0]
    n_steps = n // T_FFN
    tile = lambda width: pl.BlockSpec((T_FFN, width), lambda i: (i, 0))
    whole = lambda a: pl.BlockSpec(a.shape, lambda i: (0, 0))
    weight = lambda r, c: pl.BlockSpec((r, c), lambda i: (0, 0), pipeline_mode=pl.Buffered(1))
    scratch = [((A_CARRY, 2 * D_FF), F32), ((T_FFN, D_FF), BF16)]
    streamed = [((T_FFN, D_MODEL), F32)] * 2 + [((T_FFN, A_WIDTH), BF16), ((T_FFN, B_WIDTH), F32),
                                                ((T_FFN, C_WIDTH), BF16)]
    resident = [((D_MODEL, D_MODEL), BF16), ((D_MODEL, 2 * D_FF), BF16), ((D_FF, D_MODEL), BF16)]
    in_specs = [tile(D_MODEL), tile(A_WIDTH), tile(B_WIDTH), tile(C_WIDTH),
                whole(g_out), weight(D_MODEL, D_MODEL), whole(g_ffn), weight(D_MODEL, 2 * D_FF),
                pl.BlockSpec((None, FFN_CONV, 2 * D_FF), lambda i: (layer, 0, 0)),
                weight(D_FF, D_MODEL),
                whole(fg)]
    out_specs, out_shape, operands = [tile(D_MODEL)], [jax.ShapeDtypeStruct((n, D_MODEL), F32)], []
    if not final:
        src, dst, shape, blocks = _cast_slices(w_in, layer + 1, n_steps)
        in_specs, out_specs, out_shape, operands = in_specs + [src], out_specs + [dst], out_shape + [shape], [w_in]
        streamed += blocks
    return pl.pallas_call(
        functools.partial(_ffn_kernel, layer=layer, tiles_per_seq=seq // T_FFN, final=final),
        grid=(n_steps,),
        in_specs=in_specs,
        out_specs=out_specs,
        out_shape=out_shape,
        scratch_shapes=[pltpu.VMEM(*s) for s in scratch],
        compiler_params=pltpu.CompilerParams(dimension_semantics=("arbitrary",),
                                             vmem_limit_bytes=_vmem_limit(streamed, resident, scratch)),
        name="outproj_ffn",
    )(x, ya, yb, yc, g_out, w_out, g_ffn, w_up, cfw, w_dn, fg, *operands)


def kernel(x, norm_mix_g, w_in, conv_a_w, conv_c_w, conv_c_b, ln_c_g, ln_c_b, out_norm_g, w_out, norm_ffn_g, w_up, conv_f_w, w_down, rel_bias, final_g):
    batch, seq, d_model = x.shape
    depth = w_in.shape[0]
    assert d_model == D_MODEL and seq % T_IN == 0 and seq % T_FFN == 0 and seq % (BLK * max(DILATIONS)) == 0
    assert w_in.shape[2] == IN_COLS and w_up.shape[2] == 2 * D_FF and D_FF % FF_CHUNK == 0

    fg = final_g.reshape(1, D_MODEL)

    bias_tab = _bias_table(rel_bias)
    xf = x.reshape(batch * seq, D_MODEL)
    w_in_b = w_in[0].astype(BF16)
    for layer in range(depth):
        ya, qkv, yc, w_out_b, w_up_b, w_dn_b = _inproj(xf, layer, seq, norm_mix_g, w_in_b, conv_a_w, conv_c_w, conv_c_b,
                                                       ln_c_g, ln_c_b, out_norm_g, (w_out, w_up, w_down))
        yb = _attention(qkv, bias_tab).reshape(batch * seq, B_WIDTH)
        last = layer == depth - 1
        outs = _ffn(xf, ya, yb, yc, layer, seq, last, out_norm_g, w_out_b, norm_ffn_g, w_up_b, conv_f_w, w_dn_b, fg, w_in)
        xf = outs[0]
        if not last:
            w_in_b = outs[1]
    return xf.reshape(batch, seq, D_MODEL)
```

```python
import functools
import math

import numpy as np
import jax
import jax.numpy as jnp
from jax import lax
from jax.experimental import pallas as pl
from jax.experimental.pallas import tpu as pltpu

F32 = jnp.float32
BF16 = jnp.bfloat16

D_MODEL = 1024
D_HEAD = 64
A_WIDTH = 256
B_HEADS = 8
B_WIDTH = 512
C_WIDTH = 256
IN_COLS = 3 * A_WIDTH + 3 * B_WIDTH + 2 * C_WIDTH
QKV_LO = 3 * A_WIDTH
C_LO = QKV_LO + 3 * B_WIDTH
DILATIONS = (1, 4, 16)
N_KEYS = 128
BLK = 128
NUM_BUCKETS = 32
MAX_DISTANCE = 2048
SHORT_CONV = 3
CONFORMER_CONV = 31
FFN_CONV = 3
D_FF = 2816
EPS = 1e-6
NEG = -1e30
LOG2E = math.log2(math.e)

FFN_WEIGHTS = ((D_MODEL, D_MODEL), (D_MODEL, 2 * D_FF), (D_FF, D_MODEL))

LANES = 128
SUBLANES = 8
BF16_ROWS = 16
COMPILER_VMEM = 8 * 1024 * 1024

T_IN = 512
T_FFN = 512
CONV_ROWS = 64
FF_CHUNK = 256
A_CARRY = SUBLANES
C_CARRY = 32
BIAS_ROWS = 16


def _nbytes(shape, dtype):
    return math.prod(shape) * jnp.dtype(dtype).itemsize


def _cast_slices(w, layer, n_steps):
    rows, cols = w.shape[1:]
    assert rows % BF16_ROWS == 0
    pieces = math.gcd(n_steps, rows // BF16_ROWS)
    piece, repeat = rows // pieces, n_steps // pieces
    src = pl.BlockSpec((None, piece, cols), lambda i: (layer, i // repeat, 0))
    dst = pl.BlockSpec((piece, cols), lambda i: (i // repeat, 0))
    return src, dst, jax.ShapeDtypeStruct((rows, cols), BF16), [((piece, cols), F32), ((piece, cols), BF16)]


def _vmem_limit(streamed, resident, scratch):
    total = 2 * sum(_nbytes(*b) for b in streamed) + sum(_nbytes(*b) for b in resident + scratch)
    return total + COMPILER_VMEM


def _rms(x, g):
    return x * lax.rsqrt(jnp.mean(x * x, axis=-1, keepdims=True) + EPS) * g


def _dot(a, b):
    return jnp.dot(a, b, preferred_element_type=F32)


def _bucket_table():
    rel = np.arange(BLK)[:, None] - np.arange(2 * BLK)[None, :] + BLK
    valid = (rel >= 0) & (rel <= N_KEYS)
    max_exact = NUM_BUCKETS // 2
    out = []
    for d in DILATIONS:
        dist = np.maximum(rel, 0) * d
        d_f = np.maximum(dist, 1).astype(np.float32)
        large = max_exact + (np.log(d_f / np.float32(max_exact)) / np.float32(math.log(MAX_DISTANCE / max_exact))
                             * np.float32(NUM_BUCKETS - max_exact)).astype(np.int32)
        large = np.minimum(large, NUM_BUCKETS - 1)
        bucket = np.where(dist < max_exact, dist, large)
        out.append(np.where(valid, bucket, -1).astype(np.int32))
    return np.stack(out)


def _bias_kernel(rb_ref, bucket_ref, o_ref):
    for r0 in range(0, BLK, BIAS_ROWS):
        bucket = bucket_ref[r0:r0 + BIAS_ROWS, :]
        acc = [jnp.full(bucket.shape, NEG, F32) for _ in range(B_HEADS)]
        for b in range(NUM_BUCKETS):
            hit = bucket == b
            acc = [jnp.where(hit, rb_ref[b, h] * LOG2E, acc[h]) for h in range(B_HEADS)]
        for h in range(B_HEADS):
            o_ref[h, r0:r0 + BIAS_ROWS, :] = acc[h]


def _bias_table(rel_bias):
    buckets = jnp.asarray(_bucket_table())
    return pl.pallas_call(
        _bias_kernel,
        grid=(len(DILATIONS),),
        in_specs=[pl.BlockSpec(memory_space=pltpu.SMEM),
                  pl.BlockSpec((None, BLK, 2 * BLK), lambda br: (br, 0, 0))],
        out_specs=pl.BlockSpec((None, B_HEADS, BLK, 2 * BLK), lambda br: (br, 0, 0, 0)),
        out_shape=jax.ShapeDtypeStruct((len(DILATIONS), B_HEADS, BLK, 2 * BLK), F32),
        name="bias_table",
    )(rel_bias, buckets)


def _inproj_kernel(x_ref, gmix_ref, w_ref, caw_ref, ccw_ref, ccb_ref, lng_ref, lnb_ref, gout_ref, *refs,
                   layer, tiles_per_seq):
    one = lambda ref, lo=0, hi=None: ref[layer:layer + 1, lo:hi]
    g_mix, ccb, lng, lnb = one(gmix_ref), one(ccb_ref), one(lng_ref), one(lnb_ref)
    ga, gc = one(gout_ref, 0, A_WIDTH), one(gout_ref, A_WIDTH + B_WIDTH)
    n_w = len(FFN_WEIGHTS)
    w_f32, (ya_ref, qkv_ref, yc_ref), w_bf16 = refs[:n_w], refs[n_w:n_w + 3], refs[n_w + 3:2 * n_w + 3]
    tbuf, ubuf, ushift = refs[2 * n_w + 3:]
    t_rows = x_ref.shape[0]

    @pl.when(pl.program_id(0) % tiles_per_seq == 0)
    def _():
        tbuf[0:A_CARRY, :] = jnp.zeros((A_CARRY, A_WIDTH), F32)
        ubuf[0:C_CARRY, :] = jnp.zeros((C_CARRY, C_WIDTH), F32)

    h = _rms(x_ref[...], g_mix).astype(BF16)

    zc = _dot(h, w_ref[:, C_LO:IN_COLS])
    ubuf[C_CARRY:C_CARRY + t_rows, :] = zc[:, 0:C_WIDTH] * jax.nn.sigmoid(zc[:, C_WIDTH:2 * C_WIDTH])
    ccw = ccw_ref[...]
    first = C_CARRY - (CONFORMER_CONV - 1)
    shift_rows = C_CARRY + t_rows - SUBLANES
    for ph in range(1, SUBLANES):
        ushift[ph - 1, :, :] = ubuf[pl.ds(ph, shift_rows), :]
    for rc in range(t_rows // CONV_ROWS):
        acc = jnp.broadcast_to(ccb, (CONV_ROWS, C_WIDTH))
        for k in range(CONFORMER_CONV):
            tiles, ph = divmod(first + k, SUBLANES)
            rows = pl.ds(rc * CONV_ROWS + tiles * SUBLANES, CONV_ROWS)
            acc = acc + ccw[k:k + 1] * (ubuf[rows, :] if ph == 0 else ushift[ph - 1, rows, :])
        mu = jnp.mean(acc, axis=-1, keepdims=True)
        cen = acc - mu
        var = jnp.mean(cen * cen, axis=-1, keepdims=True)
        ln = cen * lax.rsqrt(var + EPS) * lng + lnb
        yc = ln * jax.nn.sigmoid(ln)
        yc_ref[rc * CONV_ROWS:(rc + 1) * CONV_ROWS, :] = _rms(yc, gc).astype(BF16)
    ubuf[0:C_CARRY, :] = ubuf[t_rows:t_rows + C_CARRY, :]

    za = _dot(h, w_ref[:, 0:QKV_LO])
    a_h, a_b, a_c = za[:, 0:A_WIDTH], za[:, A_WIDTH:2 * A_WIDTH], za[:, 2 * A_WIDTH:3 * A_WIDTH]
    t = a_c * a_h
    tbuf[A_CARRY:A_CARRY + t_rows, :] = t
    caw = caw_ref[...]
    conv = (caw[0:1] * tbuf[A_CARRY - 2:A_CARRY - 2 + t_rows, :]
            + caw[1:2] * tbuf[A_CARRY - 1:A_CARRY - 1 + t_rows, :]
            + caw[2:3] * t)
    tbuf[0:A_CARRY, :] = tbuf[t_rows:t_rows + A_CARRY, :]
    ya_ref[...] = _rms(a_b * conv, ga).astype(BF16)

    zqkv = _dot(h, w_ref[:, QKV_LO:C_LO])
    pairs = B_WIDTH // LANES
    for g in range(3 * pairs):
        slab = zqkv[:, g * LANES:(g + 1) * LANES]
        qkv_ref[g, :, :] = slab * (D_HEAD ** -0.5 * LOG2E) if g < pairs else slab

    for src, dst in zip(w_f32, w_bf16):
        dst[...] = src[...].astype(BF16)


def _inproj(x, layer, seq, g_mix, w_in, caw, ccw, ccb, lng, lnb, g_out, ffn_weights):
    n = x.shape[0]
    n_steps = n // T_IN
    tiles_per_seq = seq // T_IN
    slabs = 3 * B_WIDTH // LANES
    whole = lambda a: pl.BlockSpec(a.shape, lambda i: (0, 0))
    scratch = [((A_CARRY + T_IN, A_WIDTH), F32), ((C_CARRY + T_IN, C_WIDTH), F32),
               ((SUBLANES - 1, C_CARRY + T_IN - SUBLANES, C_WIDTH), F32)]
    streamed = [((T_IN, D_MODEL), F32), ((T_IN, A_WIDTH), BF16), ((slabs, T_IN, LANES), F32), ((T_IN, C_WIDTH), BF16)]
    w_in_specs, w_out_specs, w_shapes = [], [], []
    for w, shape in zip(ffn_weights, FFN_WEIGHTS):
        assert w.shape[1:] == shape
        src, dst, out, blocks = _cast_slices(w, layer, n_steps)
        w_in_specs.append(src)
        w_out_specs.append(dst)
        w_shapes.append(out)
        streamed += blocks
    return pl.pallas_call(
        functools.partial(_inproj_kernel, layer=layer, tiles_per_seq=tiles_per_seq),
        grid=(n_steps,),
        in_specs=[pl.BlockSpec((T_IN, D_MODEL), lambda i: (i, 0)),
                  whole(g_mix),
                  pl.BlockSpec((D_MODEL, IN_COLS), lambda i: (0, 0), pipeline_mode=pl.Buffered(1)),
                  pl.BlockSpec((None, SHORT_CONV, A_WIDTH), lambda i: (layer, 0, 0)),
                  pl.BlockSpec((None, CONFORMER_CONV, C_WIDTH), lambda i: (layer, 0, 0)),
                  whole(ccb), whole(lng), whole(lnb), whole(g_out)] + w_in_specs,
        out_specs=[pl.BlockSpec((T_IN, A_WIDTH), lambda i: (i, 0)),
                   pl.BlockSpec((None, slabs, T_IN, LANES),
                                lambda i: (i // tiles_per_seq, 0, i % tiles_per_seq, 0)),
                   pl.BlockSpec((T_IN, C_WIDTH), lambda i: (i, 0))] + w_out_specs,
        out_shape=[jax.ShapeDtypeStruct((n, A_WIDTH), BF16),
                   jax.ShapeDtypeStruct((n // seq, slabs, seq, LANES), F32),
                   jax.ShapeDtypeStruct((n, C_WIDTH), BF16)] + w_shapes,
        scratch_shapes=[pltpu.VMEM(*s) for s in scratch],
        compiler_params=pltpu.CompilerParams(
            dimension_semantics=("arbitrary",),
            vmem_limit_bytes=_vmem_limit(streamed, [((D_MODEL, IN_COLS), BF16)], scratch)),
        name="inproj_mix",
    )(x, g_mix, w_in, caw, ccw, ccb, lng, lnb, g_out, *ffn_weights)


def _attn_kernel(q_ref, k_ref, v_ref, bias_ref, o_ref, g4q, g4k, g4v, qs, ks, v0s, v1s, *stats):
    seq = q_ref.shape[0]
    n_blocks = seq // BLK
    quarter = seq // 4
    low = lax.broadcasted_iota(jnp.int32, (1, LANES), 1) < D_HEAD
    nt = (((1,), (1,)), ((), ()))
    head_lanes = (low, jnp.logical_not(low))
    v_ones = (v0s, v1s)

    def fill(dst, qv, kv, vv):
        qs[dst, :] = qv.astype(BF16)
        ks[dst, :] = kv.astype(BF16)
        v0s[dst, :] = jnp.where(low, vv, 1.0).astype(BF16)
        v1s[dst, :] = jnp.where(low, 1.0, vv).astype(BF16)

    for br, d in enumerate(DILATIONS):
        if d == 1:
            for c in range(4):
                rows = pl.ds(c * quarter, quarter)
                fill(rows, q_ref[rows, :], k_ref[rows, :], v_ref[rows, :])
        elif d == 4:
            for r in range(4):
                src, dst = pl.ds(r, quarter, stride=4), pl.ds(r * quarter, quarter)
                qv, kv, vv = q_ref[src, :], k_ref[src, :], v_ref[src, :]
                g4q[dst, :], g4k[dst, :], g4v[dst, :] = qv, kv, vv
                fill(dst, qv, kv, vv)
        else:
            for r in range(16):
                src, dst = pl.ds((r % 4) * quarter + r // 4, BLK, stride=4), pl.ds(r * BLK, BLK)
                fill(dst, g4q[src, :], g4k[src, :], g4v[src, :])

        per_class = seq // d // BLK
        blocks = []
        for b in range(n_blocks):
            r, j = divmod(b, per_class)
            rows = pl.ds(b * BLK, BLK)
            keys = rows if j == 0 else pl.ds((b - 1) * BLK, 2 * BLK)
            cols = pl.ds(BLK, BLK) if j == 0 else pl.ds(0, 2 * BLK)
            out = pl.ds((r % 4) * quarter + r // 4, BLK, stride=4) if d == 16 else rows
            blocks.append((rows, keys, cols, out))

        scores = []
        for rows, keys, cols, out in blocks:
            for hh in range(2):
                q_head = jnp.where(head_lanes[hh], qs[rows, :], jnp.zeros((), BF16))
                scores.append(lax.dot_general(q_head, ks[keys, :], nt, preferred_element_type=F32)
                              + bias_ref[br, hh, :, cols])
        probs = []
        for i, (rows, keys, cols, out) in enumerate(blocks):
            for hh in range(2):
                s = scores[2 * i + hh]
                m = jnp.max(s, axis=-1, keepdims=True)
                probs.append(jnp.exp2(s - m).astype(BF16))
                stats[2 * (2 * br + hh) + 1][out, :] = jnp.broadcast_to(m, (BLK, LANES))
        for i, (rows, keys, cols, out) in enumerate(blocks):
            for hh in range(2):
                stats[2 * (2 * br + hh)][out, :] = _dot(probs[2 * i + hh], v_ones[hh][keys, :])

    for t in range(n_blocks):
        r4, n = divmod(t, 4)
        rows4 = pl.ds(t * BLK, BLK)
        nat = pl.ds(4 * BLK * n + r4, BLK, stride=4)
        y = []
        for hh in range(2):
            where = (nat, rows4, rows4)
            ms = [stats[2 * (2 * br + hh) + 1][where[br], :] for br in range(len(DILATIONS))]
            m_all = jnp.maximum(jnp.maximum(ms[0], ms[1]), ms[2])
            tot = jnp.zeros((BLK, LANES), F32)
            for br in range(len(DILATIONS)):
                tot = tot + jnp.exp2(ms[br] - m_all) * stats[2 * (2 * br + hh)][where[br], :]
            y.append(tot / pltpu.roll(tot, D_HEAD, axis=1))
        o_ref[nat, :] = jnp.where(low, y[0], y[1])


def _attention(qkv, bias_tab):
    batch, _, seq, _ = qkv.shape
    pairs = B_WIDTH // LANES
    n_br = len(DILATIONS)
    blk = lambda off: pl.BlockSpec((None, None, seq, LANES), lambda g, b: (b, off * pairs + g, 0, 0))
    scratch = [((seq, LANES), F32)] * 3 + [((seq, LANES), BF16)] * 4 + [((seq, LANES), F32)] * (4 * n_br)
    streamed = [((seq, LANES), F32)] * 4 + [((n_br, 2, BLK, 2 * BLK), F32)]
    return pl.pallas_call(
        _attn_kernel,
        grid=(pairs, batch),
        in_specs=[blk(0), blk(1), blk(2),
                  pl.BlockSpec((n_br, 2, BLK, 2 * BLK), lambda g, b: (0, g, 0, 0))],
        out_specs=pl.BlockSpec((None, seq, LANES), lambda g, b: (b, 0, g)),
        out_shape=jax.ShapeDtypeStruct((batch, seq, B_WIDTH), F32),
        scratch_shapes=[pltpu.VMEM(*s) for s in scratch],
        compiler_params=pltpu.CompilerParams(dimension_semantics=("arbitrary", "arbitrary"),
                                             vmem_limit_bytes=_vmem_limit(streamed, [], scratch)),
        name="dilated_attn",
    )(qkv, qkv, qkv, bias_tab)


def _ffn_kernel(x_ref, ya_ref, yb_ref, yc_ref, gout_ref, wout_ref, gffn_ref, wup_ref, cfw_ref, wdn_ref, fg_ref,
                *refs, layer, tiles_per_seq, final):
    g_b = gout_ref[layer:layer + 1, A_WIDTH:A_WIDTH + B_WIDTH]
    g_ffn = gffn_ref[layer:layer + 1, :]
    if final:
        o_ref, carry, act = refs
    else:
        w_next_f32, o_ref, w_next_bf16, carry, act = refs
    t_rows = x_ref.shape[0]

    @pl.when(pl.program_id(0) % tiles_per_seq == 0)
    def _():
        carry[...] = jnp.zeros((A_CARRY, 2 * D_FF), F32)

    y = jnp.concatenate([ya_ref[...], _rms(yb_ref[...], g_b).astype(BF16), yc_ref[...]], axis=-1)
    x1 = x_ref[...] + _dot(y, wout_ref[...])
    h = _rms(x1, g_ffn).astype(BF16)

    def conv3(c0):
        cols = pl.ds(c0, FF_CHUNK)
        up = _dot(h, wup_ref[:, cols])
        ext = jnp.concatenate([carry[:, cols], up], axis=0)
        carry[:, cols] = up[t_rows - A_CARRY:, :]
        w = cfw_ref[:, cols]
        return (w[0:1] * pltpu.roll(ext, 2, axis=0)[A_CARRY:, :]
                + w[1:2] * pltpu.roll(ext, 1, axis=0)[A_CARRY:, :]
                + w[2:3] * up)

    for c in range(D_FF // FF_CHUNK):
        gate = conv3(c * FF_CHUNK)
        val = conv3(D_FF + c * FF_CHUNK)
        act[:, c * FF_CHUNK:(c + 1) * FF_CHUNK] = (gate * jax.nn.sigmoid(gate) * val).astype(BF16)

    x2 = x1 + _dot(act[...], wdn_ref[...])
    if final:
        x2 = _rms(x2, fg_ref[...])
    else:
        w_next_bf16[...] = w_next_f32[...].astype(BF16)
    o_ref[...] = x2


def _ffn(x, ya, yb, yc, layer, seq, final, g_out, w_out, g_ffn, w_up, cfw, w_dn, fg, w_in):
    n = x.shape[0]
    n_steps = n // T_FFN
    tile = lambda width: pl.BlockSpec((T_FFN, width), lambda i: (i, 0))
    whole = lambda a: pl.BlockSpec(a.shape, lambda i: (0, 0))
    weight = lambda r, c: pl.BlockSpec((r, c), lambda i: (0, 0), pipeline_mode=pl.Buffered(1))
    scratch = [((A_CARRY, 2 * D_FF), F32), ((T_FFN, D_FF), BF16)]
    streamed = [((T_FFN, D_MODEL), F32)] * 2 + [((T_FFN, A_WIDTH), BF16), ((T_FFN, B_WIDTH), F32),
                                                ((T_FFN, C_WIDTH), BF16)]
    resident = [((D_MODEL, D_MODEL), BF16), ((D_MODEL, 2 * D_FF), BF16), ((D_FF, D_MODEL), BF16)]
    in_specs = [tile(D_MODEL), tile(A_WIDTH), tile(B_WIDTH), tile(C_WIDTH),
                whole(g_out), weight(D_MODEL, D_MODEL), whole(g_ffn), weight(D_MODEL, 2 * D_FF),
                pl.BlockSpec((None, FFN_CONV, 2 * D_FF), lambda i: (layer, 0, 0)),
                weight(D_FF, D_MODEL),
                whole(fg)]
    out_specs, out_shape, operands = [tile(D_MODEL)], [jax.ShapeDtypeStruct((n, D_MODEL), F32)], []
    if not final:
        src, dst, shape, blocks = _cast_slices(w_in, layer + 1, n_steps)
        in_specs, out_specs, out_shape, operands = in_specs + [src], out_specs + [dst], out_shape + [shape], [w_in]
        streamed += blocks
    return pl.pallas_call(
        functools.partial(_ffn_kernel, layer=layer, tiles_per_seq=seq // T_FFN, final=final),
        grid=(n_steps,),
        in_specs=in_specs,
        out_specs=out_specs,
        out_shape=out_shape,
        scratch_shapes=[pltpu.VMEM(*s) for s in scratch],
        compiler_params=pltpu.CompilerParams(dimension_semantics=("arbitrary",),
                                             vmem_limit_bytes=_vmem_limit(streamed, resident, scratch)),
        name="outproj_ffn",
    )(x, ya, yb, yc, g_out, w_out, g_ffn, w_up, cfw, w_dn, fg, *operands)


def kernel(x, norm_mix_g, w_in, conv_a_w, conv_c_w, conv_c_b, ln_c_g, ln_c_b, out_norm_g, w_out, norm_ffn_g, w_up, conv_f_w, w_down, rel_bias, final_g):
    batch, seq, d_model = x.shape
    depth = w_in.shape[0]
    assert d_model == D_MODEL and seq % T_IN == 0 and seq % T_FFN == 0 and seq % (BLK * max(DILATIONS)) == 0
    assert w_in.shape[2] == IN_COLS and w_up.shape[2] == 2 * D_FF and D_FF % FF_CHUNK == 0

    fg = final_g.reshape(1, D_MODEL)

    bias_tab = _bias_table(rel_bias)
    xf = x.reshape(batch * seq, D_MODEL)
    w_in_b = w_in[0].astype(BF16)
    for layer in range(depth):
        ya, qkv, yc, w_out_b, w_up_b, w_dn_b = _inproj(xf, layer, seq, norm_mix_g, w_in_b, conv_a_w, conv_c_w, conv_c_b,
                                                       ln_c_g, ln_c_b, out_norm_g, (w_out, w_up, w_down))
        yb = _attention(qkv, bias_tab).reshape(batch * seq, B_WIDTH)
        last = layer == depth - 1
        outs = _ffn(xf, ya, yb, yc, layer, seq, last, out_norm_g, w_out_b, norm_ffn_g, w_up_b, conv_f_w, w_dn_b, fg, w_in)
        xf = outs[0]
        if not last:
            w_in_b = outs[1]
    return xf.reshape(batch, seq, D_MODEL)
```

```python
import functools
import math

import numpy as np
import jax
import jax.numpy as jnp
from jax import lax
from jax.experimental import pallas as pl
from jax.experimental.pallas import tpu as pltpu

F32 = jnp.float32
BF16 = jnp.bfloat16

D_MODEL = 1024
D_HEAD = 64
A_WIDTH = 256
B_HEADS = 8
B_WIDTH = 512
C_WIDTH = 256
IN_COLS = 3 * A_WIDTH + 3 * B_WIDTH + 2 * C_WIDTH
QKV_LO = 3 * A_WIDTH
C_LO = QKV_LO + 3 * B_WIDTH
DILATIONS = (1, 4, 16)
N_KEYS = 128
BLK = 128
NUM_BUCKETS = 32
MAX_DISTANCE = 2048
SHORT_CONV = 3
CONFORMER_CONV = 31
FFN_CONV = 3
D_FF = 2816
EPS = 1e-6
NEG = -1e30
LOG2E = math.log2(math.e)

FFN_WEIGHTS = ((D_MODEL, D_MODEL), (D_MODEL, 2 * D_FF), (D_FF, D_MODEL))

LANES = 128
SUBLANES = 8
BF16_ROWS = 16
COMPILER_VMEM = 8 * 1024 * 1024

T_IN = 512
T_FFN = 1024
CONV_ROWS = 64
FF_CHUNK = 256
A_CARRY = SUBLANES
C_CARRY = 32
BIAS_ROWS = 16


def _nbytes(shape, dtype):
    return math.prod(shape) * jnp.dtype(dtype).itemsize


def _cast_slices(w, layer, n_steps):
    rows, cols = w.shape[1:]
    assert rows % BF16_ROWS == 0
    pieces = math.gcd(n_steps, rows // BF16_ROWS)
    piece, repeat = rows // pieces, n_steps // pieces
    src = pl.BlockSpec((None, piece, cols), lambda i: (layer, i // repeat, 0))
    dst = pl.BlockSpec((piece, cols), lambda i: (i // repeat, 0))
    return src, dst, jax.ShapeDtypeStruct((rows, cols), BF16), [((piece, cols), F32), ((piece, cols), BF16)]


def _vmem_limit(streamed, resident, scratch):
    total = 2 * sum(_nbytes(*b) for b in streamed) + sum(_nbytes(*b) for b in resident + scratch)
    return total + COMPILER_VMEM


def _rms(x, g):
    return x * lax.rsqrt(jnp.mean(x * x, axis=-1, keepdims=True) + EPS) * g


def _dot(a, b):
    return jnp.dot(a, b, preferred_element_type=F32)


def _bucket_table():
    rel = np.arange(BLK)[:, None] - np.arange(2 * BLK)[None, :] + BLK
    valid = (rel >= 0) & (rel <= N_KEYS)
    max_exact = NUM_BUCKETS // 2
    out = []
    for d in DILATIONS:
        dist = np.maximum(rel, 0) * d
        d_f = np.maximum(dist, 1).astype(np.float32)
        large = max_exact + (np.log(d_f / np.float32(max_exact)) / np.float32(math.log(MAX_DISTANCE / max_exact))
                             * np.float32(NUM_BUCKETS - max_exact)).astype(np.int32)
        large = np.minimum(large, NUM_BUCKETS - 1)
        bucket = np.where(dist < max_exact, dist, large)
        out.append(np.where(valid, bucket, -1).astype(np.int32))
    return np.stack(out)


def _bias_kernel(rb_ref, bucket_ref, o_ref):
    for r0 in range(0, BLK, BIAS_ROWS):
        bucket = bucket_ref[r0:r0 + BIAS_ROWS, :]
        acc = [jnp.full(bucket.shape, NEG, F32) for _ in range(B_HEADS)]
        for b in range(NUM_BUCKETS):
            hit = bucket == b
            acc = [jnp.where(hit, rb_ref[b, h] * LOG2E, acc[h]) for h in range(B_HEADS)]
        for h in range(B_HEADS):
            o_ref[h, r0:r0 + BIAS_ROWS, :] = acc[h]


def _bias_table(rel_bias):
    buckets = jnp.asarray(_bucket_table())
    return pl.pallas_call(
        _bias_kernel,
        grid=(len(DILATIONS),),
        in_specs=[pl.BlockSpec(memory_space=pltpu.SMEM),
                  pl.BlockSpec((None, BLK, 2 * BLK), lambda br: (br, 0, 0))],
        out_specs=pl.BlockSpec((None, B_HEADS, BLK, 2 * BLK), lambda br: (br, 0, 0, 0)),
        out_shape=jax.ShapeDtypeStruct((len(DILATIONS), B_HEADS, BLK, 2 * BLK), F32),
        name="bias_table",
    )(rel_bias, buckets)


def _inproj_kernel(x_ref, gmix_ref, w_ref, caw_ref, ccw_ref, ccb_ref, lng_ref, lnb_ref, gout_ref, *refs,
                   layer, tiles_per_seq):
    one = lambda ref, lo=0, hi=None: ref[layer:layer + 1, lo:hi]
    g_mix, ccb, lng, lnb = one(gmix_ref), one(ccb_ref), one(lng_ref), one(lnb_ref)
    ga, gc = one(gout_ref, 0, A_WIDTH), one(gout_ref, A_WIDTH + B_WIDTH)
    n_w = len(FFN_WEIGHTS)
    w_f32, (ya_ref, qkv_ref, yc_ref), w_bf16 = refs[:n_w], refs[n_w:n_w + 3], refs[n_w + 3:2 * n_w + 3]
    tbuf, ubuf, ushift = refs[2 * n_w + 3:]
    t_rows = x_ref.shape[0]

    @pl.when(pl.program_id(0) % tiles_per_seq == 0)
    def _():
        tbuf[0:A_CARRY, :] = jnp.zeros((A_CARRY, A_WIDTH), F32)
        ubuf[0:C_CARRY, :] = jnp.zeros((C_CARRY, C_WIDTH), F32)

    h = _rms(x_ref[...], g_mix).astype(BF16)

    zc = _dot(h, w_ref[:, C_LO:IN_COLS])
    ubuf[C_CARRY:C_CARRY + t_rows, :] = zc[:, 0:C_WIDTH] * jax.nn.sigmoid(zc[:, C_WIDTH:2 * C_WIDTH])
    ccw = ccw_ref[...]
    first = C_CARRY - (CONFORMER_CONV - 1)
    shift_rows = C_CARRY + t_rows - SUBLANES
    for ph in range(1, SUBLANES):
        ushift[ph - 1, :, :] = ubuf[pl.ds(ph, shift_rows), :]
    for rc in range(t_rows // CONV_ROWS):
        acc = jnp.broadcast_to(ccb, (CONV_ROWS, C_WIDTH))
        for k in range(CONFORMER_CONV):
            tiles, ph = divmod(first + k, SUBLANES)
            rows = pl.ds(rc * CONV_ROWS + tiles * SUBLANES, CONV_ROWS)
            acc = acc + ccw[k:k + 1] * (ubuf[rows, :] if ph == 0 else ushift[ph - 1, rows, :])
        mu = jnp.mean(acc, axis=-1, keepdims=True)
        cen = acc - mu
        var = jnp.mean(cen * cen, axis=-1, keepdims=True)
        ln = cen * lax.rsqrt(var + EPS) * lng + lnb
        yc = ln * jax.nn.sigmoid(ln)
        yc_ref[rc * CONV_ROWS:(rc + 1) * CONV_ROWS, :] = _rms(yc, gc).astype(BF16)
    ubuf[0:C_CARRY, :] = ubuf[t_rows:t_rows + C_CARRY, :]

    za = _dot(h, w_ref[:, 0:QKV_LO])
    a_h, a_b, a_c = za[:, 0:A_WIDTH], za[:, A_WIDTH:2 * A_WIDTH], za[:, 2 * A_WIDTH:3 * A_WIDTH]
    t = a_c * a_h
    tbuf[A_CARRY:A_CARRY + t_rows, :] = t
    caw = caw_ref[...]
    conv = (caw[0:1] * tbuf[A_CARRY - 2:A_CARRY - 2 + t_rows, :]
            + caw[1:2] * tbuf[A_CARRY - 1:A_CARRY - 1 + t_rows, :]
            + caw[2:3] * t)
    tbuf[0:A_CARRY, :] = tbuf[t_rows:t_rows + A_CARRY, :]
    ya_ref[...] = _rms(a_b * conv, ga).astype(BF16)

    zqkv = _dot(h, w_ref[:, QKV_LO:C_LO])
    pairs = B_WIDTH // LANES
    for g in range(3 * pairs):
        slab = zqkv[:, g * LANES:(g + 1) * LANES]
        qkv_ref[g, :, :] = slab * (D_HEAD ** -0.5 * LOG2E) if g < pairs else slab

    for src, dst in zip(w_f32, w_bf16):
        dst[...] = src[...].astype(BF16)


def _inproj(x, layer, seq, g_mix, w_in, caw, ccw, ccb, lng, lnb, g_out, ffn_weights):
    n = x.shape[0]
    n_steps = n // T_IN
    tiles_per_seq = seq // T_IN
    slabs = 3 * B_WIDTH // LANES
    whole = lambda a: pl.BlockSpec(a.shape, lambda i: (0, 0))
    scratch = [((A_CARRY + T_IN, A_WIDTH), F32), ((C_CARRY + T_IN, C_WIDTH), F32),
               ((SUBLANES - 1, C_CARRY + T_IN - SUBLANES, C_WIDTH), F32)]
    streamed = [((T_IN, D_MODEL), F32), ((T_IN, A_WIDTH), BF16), ((slabs, T_IN, LANES), F32), ((T_IN, C_WIDTH), BF16)]
    w_in_specs, w_out_specs, w_shapes = [], [], []
    for w, shape in zip(ffn_weights, FFN_WEIGHTS):
        assert w.shape[1:] == shape
        src, dst, out, blocks = _cast_slices(w, layer, n_steps)
        w_in_specs.append(src)
        w_out_specs.append(dst)
        w_shapes.append(out)
        streamed += blocks
    return pl.pallas_call(
        functools.partial(_inproj_kernel, layer=layer, tiles_per_seq=tiles_per_seq),
        grid=(n_steps,),
        in_specs=[pl.BlockSpec((T_IN, D_MODEL), lambda i: (i, 0)),
                  whole(g_mix),
                  pl.BlockSpec((D_MODEL, IN_COLS), lambda i: (0, 0), pipeline_mode=pl.Buffered(1)),
                  pl.BlockSpec((None, SHORT_CONV, A_WIDTH), lambda i: (layer, 0, 0)),
                  pl.BlockSpec((None, CONFORMER_CONV, C_WIDTH), lambda i: (layer, 0, 0)),
                  whole(ccb), whole(lng), whole(lnb), whole(g_out)] + w_in_specs,
        out_specs=[pl.BlockSpec((T_IN, A_WIDTH), lambda i: (i, 0)),
                   pl.BlockSpec((None, slabs, T_IN, LANES),
                                lambda i: (i // tiles_per_seq, 0, i % tiles_per_seq, 0)),
                   pl.BlockSpec((T_IN, C_WIDTH), lambda i: (i, 0))] + w_out_specs,
        out_shape=[jax.ShapeDtypeStruct((n, A_WIDTH), BF16),
                   jax.ShapeDtypeStruct((n // seq, slabs, seq, LANES), F32),
                   jax.ShapeDtypeStruct((n, C_WIDTH), BF16)] + w_shapes,
        scratch_shapes=[pltpu.VMEM(*s) for s in scratch],
        compiler_params=pltpu.CompilerParams(
            dimension_semantics=("arbitrary",),
            vmem_limit_bytes=_vmem_limit(streamed, [((D_MODEL, IN_COLS), BF16)], scratch)),
        name="inproj_mix",
    )(x, g_mix, w_in, caw, ccw, ccb, lng, lnb, g_out, *ffn_weights)


def _attn_kernel(q_ref, k_ref, v_ref, bias_ref, o_ref, g4q, g4k, g4v, qs, ks, v0s, v1s, *stats):
    seq = q_ref.shape[0]
    n_blocks = seq // BLK
    quarter = seq // 4
    low = lax.broadcasted_iota(jnp.int32, (1, LANES), 1) < D_HEAD
    nt = (((1,), (1,)), ((), ()))
    head_lanes = (low, jnp.logical_not(low))
    v_ones = (v0s, v1s)

    def fill(dst, qv, kv, vv):
        qs[dst, :] = qv.astype(BF16)
        ks[dst, :] = kv.astype(BF16)
        v0s[dst, :] = jnp.where(low, vv, 1.0).astype(BF16)
        v1s[dst, :] = jnp.where(low, 1.0, vv).astype(BF16)

    for br, d in enumerate(DILATIONS):
        if d == 1:
            for c in range(4):
                rows = pl.ds(c * quarter, quarter)
                fill(rows, q_ref[rows, :], k_ref[rows, :], v_ref[rows, :])
        elif d == 4:
            for r in range(4):
                src, dst = pl.ds(r, quarter, stride=4), pl.ds(r * quarter, quarter)
                qv, kv, vv = q_ref[src, :], k_ref[src, :], v_ref[src, :]
                g4q[dst, :], g4k[dst, :], g4v[dst, :] = qv, kv, vv
                fill(dst, qv, kv, vv)
        else:
            for r in range(16):
                src, dst = pl.ds((r % 4) * quarter + r // 4, BLK, stride=4), pl.ds(r * BLK, BLK)
                fill(dst, g4q[src, :], g4k[src, :], g4v[src, :])

        per_class = seq // d // BLK
        blocks = []
        for b in range(n_blocks):
            r, j = divmod(b, per_class)
            rows = pl.ds(b * BLK, BLK)
            keys = rows if j == 0 else pl.ds((b - 1) * BLK, 2 * BLK)
            cols = pl.ds(BLK, BLK) if j == 0 else pl.ds(0, 2 * BLK)
            out = pl.ds((r % 4) * quarter + r // 4, BLK, stride=4) if d == 16 else rows
            blocks.append((rows, keys, cols, out))

        scores = []
        for rows, keys, cols, out in blocks:
            for hh in range(2):
                q_head = jnp.where(head_lanes[hh], qs[rows, :], jnp.zeros((), BF16))
                scores.append(lax.dot_general(q_head, ks[keys, :], nt, preferred_element_type=F32)
                              + bias_ref[br, hh, :, cols])
        probs = []
        for i, (rows, keys, cols, out) in enumerate(blocks):
            for hh in range(2):
                s = scores[2 * i + hh]
                m = jnp.max(s, axis=-1, keepdims=True)
                probs.append(jnp.exp2(s - m).astype(BF16))
                stats[2 * (2 * br + hh) + 1][out, :] = jnp.broadcast_to(m, (BLK, LANES))
        for i, (rows, keys, cols, out) in enumerate(blocks):
            for hh in range(2):
                stats[2 * (2 * br + hh)][out, :] = _dot(probs[2 * i + hh], v_ones[hh][keys, :])

    for t in range(n_blocks):
        r4, n = divmod(t, 4)
        rows4 = pl.ds(t * BLK, BLK)
        nat = pl.ds(4 * BLK * n + r4, BLK, stride=4)
        y = []
        for hh in range(2):
            where = (nat, rows4, rows4)
            ms = [stats[2 * (2 * br + hh) + 1][where[br], :] for br in range(len(DILATIONS))]
            m_all = jnp.maximum(jnp.maximum(ms[0], ms[1]), ms[2])
            tot = jnp.zeros((BLK, LANES), F32)
            for br in range(len(DILATIONS)):
                tot = tot + jnp.exp2(ms[br] - m_all) * stats[2 * (2 * br + hh)][where[br], :]
            y.append(tot / pltpu.roll(tot, D_HEAD, axis=1))
        o_ref[nat, :] = jnp.where(low, y[0], y[1])


def _attention(qkv, bias_tab):
    batch, _, seq, _ = qkv.shape
    pairs = B_WIDTH // LANES
    n_br = len(DILATIONS)
    blk = lambda off: pl.BlockSpec((None, None, seq, LANES), lambda g, b: (b, off * pairs + g, 0, 0))
    scratch = [((seq, LANES), F32)] * 3 + [((seq, LANES), BF16)] * 4 + [((seq, LANES), F32)] * (4 * n_br)
    streamed = [((seq, LANES), F32)] * 4 + [((n_br, 2, BLK, 2 * BLK), F32)]
    return pl.pallas_call(
        _attn_kernel,
        grid=(pairs, batch),
        in_specs=[blk(0), blk(1), blk(2),
                  pl.BlockSpec((n_br, 2, BLK, 2 * BLK), lambda g, b: (0, g, 0, 0))],
        out_specs=pl.BlockSpec((None, seq, LANES), lambda g, b: (b, 0, g)),
        out_shape=jax.ShapeDtypeStruct((batch, seq, B_WIDTH), F32),
        scratch_shapes=[pltpu.VMEM(*s) for s in scratch],
        compiler_params=pltpu.CompilerParams(dimension_semantics=("arbitrary", "arbitrary"),
                                             vmem_limit_bytes=_vmem_limit(streamed, [], scratch)),
        name="dilated_attn",
    )(qkv, qkv, qkv, bias_tab)


def _ffn_kernel(x_ref, ya_ref, yb_ref, yc_ref, gout_ref, wout_ref, gffn_ref, wup_ref, cfw_ref, wdn_ref, fg_ref,
                *refs, layer, tiles_per_seq, final):
    g_b = gout_ref[layer:layer + 1, A_WIDTH:A_WIDTH + B_WIDTH]
    g_ffn = gffn_ref[layer:layer + 1, :]
    if final:
        o_ref, carry, act = refs
    else:
        w_next_f32, o_ref, w_next_bf16, carry, act = refs
    t_rows = x_ref.shape[0]

    @pl.when(pl.program_id(0) % tiles_per_seq == 0)
    def _():
        carry[...] = jnp.zeros((A_CARRY, 2 * D_FF), F32)

    y = jnp.concatenate([ya_ref[...], _rms(yb_ref[...], g_b).astype(BF16), yc_ref[...]], axis=-1)
    x1 = x_ref[...] + _dot(y, wout_ref[...])
    h = _rms(x1, g_ffn).astype(BF16)

    def conv3(c0):
        cols = pl.ds(c0, FF_CHUNK)
        up = _dot(h, wup_ref[:, cols])
        ext = jnp.concatenate([carry[:, cols], up], axis=0)
        carry[:, cols] = up[t_rows - A_CARRY:, :]
        w = cfw_ref[:, cols]
        return (w[0:1] * pltpu.roll(ext, 2, axis=0)[A_CARRY:, :]
                + w[1:2] * pltpu.roll(ext, 1, axis=0)[A_CARRY:, :]
                + w[2:3] * up)

    for c in range(D_FF // FF_CHUNK):
        gate = conv3(c * FF_CHUNK)
        val = conv3(D_FF + c * FF_CHUNK)
        act[:, c * FF_CHUNK:(c + 1) * FF_CHUNK] = (gate * jax.nn.sigmoid(gate) * val).astype(BF16)

    x2 = x1 + _dot(act[...], wdn_ref[...])
    if final:
        x2 = _rms(x2, fg_ref[...])
    else:
        w_next_bf16[...] = w_next_f32[...].astype(BF16)
    o_ref[...] = x2


def _ffn(x, ya, yb, yc, layer, seq, final, g_out, w_out, g_ffn, w_up, cfw, w_dn, fg, w_in):
    n = x.shape[0]
    n_steps = n // T_FFN
    tile = lambda width: pl.BlockSpec((T_FFN, width), lambda i: (i, 0))
    whole = lambda a: pl.BlockSpec(a.shape, lambda i: (0, 0))
    weight = lambda r, c: pl.BlockSpec((r, c), lambda i: (0, 0), pipeline_mode=pl.Buffered(1))
    scratch = [((A_CARRY, 2 * D_FF), F32), ((T_FFN, D_FF), BF16)]
    streamed = [((T_FFN, D_MODEL), F32)] * 2 + [((T_FFN, A_WIDTH), BF16), ((T_FFN, B_WIDTH), F32),
                                                ((T_FFN, C_WIDTH), BF16)]
    resident = [((D_MODEL, D_MODEL), BF16), ((D_MODEL, 2 * D_FF), BF16), ((D_FF, D_MODEL), BF16)]
    in_specs = [tile(D_MODEL), tile(A_WIDTH), tile(B_WIDTH), tile(C_WIDTH),
                whole(g_out), weight(D_MODEL, D_MODEL), whole(g_ffn), weight(D_MODEL, 2 * D_FF),
                pl.BlockSpec((None, FFN_CONV, 2 * D_FF), lambda i: (layer, 0, 0)),
                weight(D_FF, D_MODEL),
                whole(fg)]
    out_specs, out_shape, operands = [tile(D_MODEL)], [jax.ShapeDtypeStruct((n, D_MODEL), F32)], []
    if not final:
        src, dst, shape, blocks = _cast_slices(w_in, layer + 1, n_steps)
        in_specs, out_specs, out_shape, operands = in_specs + [src], out_specs + [dst], out_shape + [shape], [w_in]
        streamed += blocks
    return pl.pallas_call(
        functools.partial(_ffn_kernel, layer=layer, tiles_per_seq=seq // T_FFN, final=final),
        grid=(n_steps,),
        in_specs=in_specs,
        out_specs=out_specs,
        out_shape=out_shape,
        scratch_shapes=[pltpu.VMEM(*s) for s in scratch],
        compiler_params=pltpu.CompilerParams(dimension_semantics=("arbitrary",),
                                             vmem_limit_bytes=_vmem_limit(streamed, resident, scratch)),
        name="outproj_ffn",
    )(x, ya, yb, yc, g_out, w_out, g_ffn, w_up, cfw, w_dn, fg, *operands)


def kernel(x, norm_mix_g, w_in, conv_a_w, conv_c_w, conv_c_b, ln_c_g, ln_c_b, out_norm_g, w_out, norm_ffn_g, w_up, conv_f_w, w_down, rel_bias, final_g):
    batch, seq, d_model = x.shape
    depth = w_in.shape[0]
    assert d_model == D_MODEL and seq % T_IN == 0 and seq % T_FFN == 0 and seq % (BLK * max(DILATIONS)) == 0
    assert w_in.shape[2] == IN_COLS and w_up.shape[2] == 2 * D_FF and D_FF % FF_CHUNK == 0

    fg = final_g.reshape(1, D_MODEL)

    bias_tab = _bias_table(rel_bias)
    xf = x.reshape(batch * seq, D_MODEL)
    w_in_b = w_in[0].astype(BF16)
    for layer in range(depth):
        ya, qkv, yc, w_out_b, w_up_b, w_dn_b = _inproj(xf, layer, seq, norm_mix_g, w_in_b, conv_a_w, conv_c_w, conv_c_b,
                                                       ln_c_g, ln_c_b, out_norm_g, (w_out, w_up, w_down))
        yb = _attention(qkv, bias_tab).reshape(batch * seq, B_WIDTH)
        last = layer == depth - 1
        outs = _ffn(xf, ya, yb, yc, layer, seq, last, out_norm_g, w_out_b, norm_ffn_g, w_up_b, conv_f_w, w_dn_b, fg, w_in)
        xf = outs[0]
        if not last:
            w_in_b = outs[1]
    return xf.reshape(batch, seq, D_MODEL)
```

```python
import functools
import math

import numpy as np
import jax
import jax.numpy as jnp
from jax import lax
from jax.experimental import pallas as pl
from jax.experimental.pallas import tpu as pltpu

F32 = jnp.float32
BF16 = jnp.bfloat16

D_MODEL = 1024
D_HEAD = 64
A_WIDTH = 256
B_HEADS = 8
B_WIDTH = 512
C_WIDTH = 256
IN_COLS = 3 * A_WIDTH + 3 * B_WIDTH + 2 * C_WIDTH
QKV_LO = 3 * A_WIDTH
C_LO = QKV_LO + 3 * B_WIDTH
DILATIONS = (1, 4, 16)
N_KEYS = 128
BLK = 128
NUM_BUCKETS = 32
MAX_DISTANCE = 2048
SHORT_CONV = 3
CONFORMER_CONV = 31
FFN_CONV = 3
D_FF = 2816
EPS = 1e-6
NEG = -1e30
LOG2E = math.log2(math.e)

FFN_WEIGHTS = ((D_MODEL, D_MODEL), (D_MODEL, 2 * D_FF), (D_FF, D_MODEL))

LANES = 128
SUBLANES = 8
BF16_ROWS = 16
COMPILER_VMEM = 4 * 1024 * 1024

T_IN = 512
T_FFN = 512
CONV_ROWS = 64
FF_CHUNK = 256
A_CARRY = SUBLANES
C_CARRY = 32
BIAS_ROWS = 16


def _nbytes(shape, dtype):
    return math.prod(shape) * jnp.dtype(dtype).itemsize


def _cast_slices(w, layer, n_steps):
    rows, cols = w.shape[1:]
    assert rows % BF16_ROWS == 0
    pieces = math.gcd(n_steps, rows // BF16_ROWS)
    piece, repeat = rows // pieces, n_steps // pieces
    src = pl.BlockSpec((None, piece, cols), lambda i: (layer, i // repeat, 0))
    dst = pl.BlockSpec((piece, cols), lambda i: (i // repeat, 0))
    return src, dst, jax.ShapeDtypeStruct((rows, cols), BF16), [((piece, cols), F32), ((piece, cols), BF16)]


def _vmem_limit(streamed, resident, scratch):
    total = 2 * sum(_nbytes(*b) for b in streamed) + sum(_nbytes(*b) for b in resident + scratch)
    return total + COMPILER_VMEM


def _rms(x, g):
    return x * lax.rsqrt(jnp.mean(x * x, axis=-1, keepdims=True) + EPS) * g


def _dot(a, b):
    return jnp.dot(a, b, preferred_element_type=F32)


def _bucket_table():
    rel = np.arange(BLK)[:, None] - np.arange(2 * BLK)[None, :] + BLK
    valid = (rel >= 0) & (rel <= N_KEYS)
    max_exact = NUM_BUCKETS // 2
    out = []
    for d in DILATIONS:
        dist = np.maximum(rel, 0) * d
        d_f = np.maximum(dist, 1).astype(np.float32)
        large = max_exact + (np.log(d_f / np.float32(max_exact)) / np.float32(math.log(MAX_DISTANCE / max_exact))
                             * np.float32(NUM_BUCKETS - max_exact)).astype(np.int32)
        large = np.minimum(large, NUM_BUCKETS - 1)
        bucket = np.where(dist < max_exact, dist, large)
        out.append(np.where(valid, bucket, -1).astype(np.int32))
    return np.stack(out)


def _bias_kernel(rb_ref, bucket_ref, o_ref):
    for r0 in range(0, BLK, BIAS_ROWS):
        bucket = bucket_ref[r0:r0 + BIAS_ROWS, :]
        acc = [jnp.full(bucket.shape, NEG, F32) for _ in range(B_HEADS)]
        for b in range(NUM_BUCKETS):
            hit = bucket == b
            acc = [jnp.where(hit, rb_ref[b, h] * LOG2E, acc[h]) for h in range(B_HEADS)]
        for h in range(B_HEADS):
            o_ref[h, r0:r0 + BIAS_ROWS, :] = acc[h]


def _bias_table(rel_bias):
    buckets = jnp.asarray(_bucket_table())
    return pl.pallas_call(
        _bias_kernel,
        grid=(len(DILATIONS),),
        in_specs=[pl.BlockSpec(memory_space=pltpu.SMEM),
                  pl.BlockSpec((None, BLK, 2 * BLK), lambda br: (br, 0, 0))],
        out_specs=pl.BlockSpec((None, B_HEADS, BLK, 2 * BLK), lambda br: (br, 0, 0, 0)),
        out_shape=jax.ShapeDtypeStruct((len(DILATIONS), B_HEADS, BLK, 2 * BLK), F32),
        name="bias_table",
    )(rel_bias, buckets)


def _inproj_kernel(x_ref, gmix_ref, w_ref, caw_ref, ccw_ref, ccb_ref, lng_ref, lnb_ref, gout_ref, *refs,
                   layer, tiles_per_seq):
    one = lambda ref, lo=0, hi=None: ref[layer:layer + 1, lo:hi]
    g_mix, ccb, lng, lnb = one(gmix_ref), one(ccb_ref), one(lng_ref), one(lnb_ref)
    ga, gc = one(gout_ref, 0, A_WIDTH), one(gout_ref, A_WIDTH + B_WIDTH)
    n_w = len(FFN_WEIGHTS)
    w_f32, (ya_ref, qkv_ref, yc_ref), w_bf16 = refs[:n_w], refs[n_w:n_w + 3], refs[n_w + 3:2 * n_w + 3]
    tbuf, ubuf, ushift = refs[2 * n_w + 3:]
    t_rows = x_ref.shape[0]

    @pl.when(pl.program_id(0) % tiles_per_seq == 0)
    def _():
        tbuf[0:A_CARRY, :] = jnp.zeros((A_CARRY, A_WIDTH), F32)
        ubuf[0:C_CARRY, :] = jnp.zeros((C_CARRY, C_WIDTH), F32)

    h = _rms(x_ref[...], g_mix).astype(BF16)

    zc = _dot(h, w_ref[:, C_LO:IN_COLS])
    ubuf[C_CARRY:C_CARRY + t_rows, :] = zc[:, 0:C_WIDTH] * jax.nn.sigmoid(zc[:, C_WIDTH:2 * C_WIDTH])
    ccw = ccw_ref[...]
    first = C_CARRY - (CONFORMER_CONV - 1)
    shift_rows = C_CARRY + t_rows - SUBLANES
    for ph in range(1, SUBLANES):
        ushift[ph - 1, :, :] = ubuf[pl.ds(ph, shift_rows), :]
    for rc in range(t_rows // CONV_ROWS):
        acc = jnp.broadcast_to(ccb, (CONV_ROWS, C_WIDTH))
        for k in range(CONFORMER_CONV):
            tiles, ph = divmod(first + k, SUBLANES)
            rows = pl.ds(rc * CONV_ROWS + tiles * SUBLANES, CONV_ROWS)
            acc = acc + ccw[k:k + 1] * (ubuf[rows, :] if ph == 0 else ushift[ph - 1, rows, :])
        mu = jnp.mean(acc, axis=-1, keepdims=True)
        cen = acc - mu
        var = jnp.mean(cen * cen, axis=-1, keepdims=True)
        ln = cen * lax.rsqrt(var + EPS) * lng + lnb
        yc = ln * jax.nn.sigmoid(ln)
        yc_ref[rc * CONV_ROWS:(rc + 1) * CONV_ROWS, :] = _rms(yc, gc).astype(BF16)
    ubuf[0:C_CARRY, :] = ubuf[t_rows:t_rows + C_CARRY, :]

    za = _dot(h, w_ref[:, 0:QKV_LO])
    a_h, a_b, a_c = za[:, 0:A_WIDTH], za[:, A_WIDTH:2 * A_WIDTH], za[:, 2 * A_WIDTH:3 * A_WIDTH]
    t = a_c * a_h
    tbuf[A_CARRY:A_CARRY + t_rows, :] = t
    caw = caw_ref[...]
    conv = (caw[0:1] * tbuf[A_CARRY - 2:A_CARRY - 2 + t_rows, :]
            + caw[1:2] * tbuf[A_CARRY - 1:A_CARRY - 1 + t_rows, :]
            + caw[2:3] * t)
    tbuf[0:A_CARRY, :] = tbuf[t_rows:t_rows + A_CARRY, :]
    ya_ref[...] = _rms(a_b * conv, ga).astype(BF16)

    zqkv = _dot(h, w_ref[:, QKV_LO:C_LO])
    pairs = B_WIDTH // LANES
    for g in range(3 * pairs):
        slab = zqkv[:, g * LANES:(g + 1) * LANES]
        qkv_ref[g, :, :] = slab * (D_HEAD ** -0.5 * LOG2E) if g < pairs else slab

    for src, dst in zip(w_f32, w_bf16):
        dst[...] = src[...].astype(BF16)


def _inproj(x, layer, seq, g_mix, w_in, caw, ccw, ccb, lng, lnb, g_out, ffn_weights):
    n = x.shape[0]
    n_steps = n // T_IN
    tiles_per_seq = seq // T_IN
    slabs = 3 * B_WIDTH // LANES
    whole = lambda a: pl.BlockSpec(a.shape, lambda i: (0, 0))
    scratch = [((A_CARRY + T_IN, A_WIDTH), F32), ((C_CARRY + T_IN, C_WIDTH), F32),
               ((SUBLANES - 1, C_CARRY + T_IN - SUBLANES, C_WIDTH), F32)]
    streamed = [((T_IN, D_MODEL), F32), ((T_IN, A_WIDTH), BF16), ((slabs, T_IN, LANES), F32), ((T_IN, C_WIDTH), BF16)]
    w_in_specs, w_out_specs, w_shapes = [], [], []
    for w, shape in zip(ffn_weights, FFN_WEIGHTS):
        assert w.shape[1:] == shape
        src, dst, out, blocks = _cast_slices(w, layer, n_steps)
        w_in_specs.append(src)
        w_out_specs.append(dst)
        w_shapes.append(out)
        streamed += blocks
    return pl.pallas_call(
        functools.partial(_inproj_kernel, layer=layer, tiles_per_seq=tiles_per_seq),
        grid=(n_steps,),
        in_specs=[pl.BlockSpec((T_IN, D_MODEL), lambda i: (i, 0)),
                  whole(g_mix),
                  pl.BlockSpec((D_MODEL, IN_COLS), lambda i: (0, 0), pipeline_mode=pl.Buffered(1)),
                  pl.BlockSpec((None, SHORT_CONV, A_WIDTH), lambda i: (layer, 0, 0)),
                  pl.BlockSpec((None, CONFORMER_CONV, C_WIDTH), lambda i: (layer, 0, 0)),
                  whole(ccb), whole(lng), whole(lnb), whole(g_out)] + w_in_specs,
        out_specs=[pl.BlockSpec((T_IN, A_WIDTH), lambda i: (i, 0)),
                   pl.BlockSpec((None, slabs, T_IN, LANES),
                                lambda i: (i // tiles_per_seq, 0, i % tiles_per_seq, 0)),
                   pl.BlockSpec((T_IN, C_WIDTH), lambda i: (i, 0))] + w_out_specs,
        out_shape=[jax.ShapeDtypeStruct((n, A_WIDTH), BF16),
                   jax.ShapeDtypeStruct((n // seq, slabs, seq, LANES), F32),
                   jax.ShapeDtypeStruct((n, C_WIDTH), BF16)] + w_shapes,
        scratch_shapes=[pltpu.VMEM(*s) for s in scratch],
        compiler_params=pltpu.CompilerParams(
            dimension_semantics=("arbitrary",),
            vmem_limit_bytes=_vmem_limit(streamed, [((D_MODEL, IN_COLS), BF16)], scratch)),
        name="inproj_mix",
    )(x, g_mix, w_in, caw, ccw, ccb, lng, lnb, g_out, *ffn_weights)


def _attn_kernel(q_ref, k_ref, v_ref, bias_ref, o_ref, g4q, g4k, g4v, qs, ks, v0s, v1s, *stats):
    seq = q_ref.shape[0]
    n_blocks = seq // BLK
    quarter = seq // 4
    low = lax.broadcasted_iota(jnp.int32, (1, LANES), 1) < D_HEAD
    nt = (((1,), (1,)), ((), ()))
    head_lanes = (low, jnp.logical_not(low))
    v_ones = (v0s, v1s)

    def fill(dst, qv, kv, vv):
        qs[dst, :] = qv.astype(BF16)
        ks[dst, :] = kv.astype(BF16)
        v0s[dst, :] = jnp.where(low, vv, 1.0).astype(BF16)
        v1s[dst, :] = jnp.where(low, 1.0, vv).astype(BF16)

    for br, d in enumerate(DILATIONS):
        if d == 1:
            for c in range(4):
                rows = pl.ds(c * quarter, quarter)
                fill(rows, q_ref[rows, :], k_ref[rows, :], v_ref[rows, :])
        elif d == 4:
            for r in range(4):
                src, dst = pl.ds(r, quarter, stride=4), pl.ds(r * quarter, quarter)
                qv, kv, vv = q_ref[src, :], k_ref[src, :], v_ref[src, :]
                g4q[dst, :], g4k[dst, :], g4v[dst, :] = qv, kv, vv
                fill(dst, qv, kv, vv)
        else:
            for r in range(16):
                src, dst = pl.ds((r % 4) * quarter + r // 4, BLK, stride=4), pl.ds(r * BLK, BLK)
                fill(dst, g4q[src, :], g4k[src, :], g4v[src, :])

        per_class = seq // d // BLK
        blocks = []
        for b in range(n_blocks):
            r, j = divmod(b, per_class)
            rows = pl.ds(b * BLK, BLK)
            keys = rows if j == 0 else pl.ds((b - 1) * BLK, 2 * BLK)
            cols = pl.ds(BLK, BLK) if j == 0 else pl.ds(0, 2 * BLK)
            out = pl.ds((r % 4) * quarter + r // 4, BLK, stride=4) if d == 16 else rows
            blocks.append((rows, keys, cols, out))

        scores = []
        for rows, keys, cols, out in blocks:
            for hh in range(2):
                q_head = jnp.where(head_lanes[hh], qs[rows, :], jnp.zeros((), BF16))
                scores.append(lax.dot_general(q_head, ks[keys, :], nt, preferred_element_type=F32)
                              + bias_ref[br, hh, :, cols])
        probs = []
        for i, (rows, keys, cols, out) in enumerate(blocks):
            for hh in range(2):
                s = scores[2 * i + hh]
                m = jnp.max(s, axis=-1, keepdims=True)
                probs.append(jnp.exp2(s - m).astype(BF16))
                stats[2 * (2 * br + hh) + 1][out, :] = jnp.broadcast_to(m, (BLK, LANES))
        for i, (rows, keys, cols, out) in enumerate(blocks):
            for hh in range(2):
                stats[2 * (2 * br + hh)][out, :] = _dot(probs[2 * i + hh], v_ones[hh][keys, :])

    for t in range(n_blocks):
        r4, n = divmod(t, 4)
        rows4 = pl.ds(t * BLK, BLK)
        nat = pl.ds(4 * BLK * n + r4, BLK, stride=4)
        y = []
        for hh in range(2):
            where = (nat, rows4, rows4)
            ms = [stats[2 * (2 * br + hh) + 1][where[br], :] for br in range(len(DILATIONS))]
            m_all = jnp.maximum(jnp.maximum(ms[0], ms[1]), ms[2])
            tot = jnp.zeros((BLK, LANES), F32)
            for br in range(len(DILATIONS)):
                tot = tot + jnp.exp2(ms[br] - m_all) * stats[2 * (2 * br + hh)][where[br], :]
            y.append(tot / pltpu.roll(tot, D_HEAD, axis=1))
        o_ref[nat, :] = jnp.where(low, y[0], y[1])


def _attention(qkv, bias_tab):
    batch, _, seq, _ = qkv.shape
    pairs = B_WIDTH // LANES
    n_br = len(DILATIONS)
    blk = lambda off: pl.BlockSpec((None, None, seq, LANES), lambda g, b: (b, off * pairs + g, 0, 0))
    scratch = [((seq, LANES), F32)] * 3 + [((seq, LANES), BF16)] * 4 + [((seq, LANES), F32)] * (4 * n_br)
    streamed = [((seq, LANES), F32)] * 4 + [((n_br, 2, BLK, 2 * BLK), F32)]
    return pl.pallas_call(
        _attn_kernel,
        grid=(pairs, batch),
        in_specs=[blk(0), blk(1), blk(2),
                  pl.BlockSpec((n_br, 2, BLK, 2 * BLK), lambda g, b: (0, g, 0, 0))],
        out_specs=pl.BlockSpec((None, seq, LANES), lambda g, b: (b, 0, g)),
        out_shape=jax.ShapeDtypeStruct((batch, seq, B_WIDTH), F32),
        scratch_shapes=[pltpu.VMEM(*s) for s in scratch],
        compiler_params=pltpu.CompilerParams(dimension_semantics=("arbitrary", "arbitrary"),
                                             vmem_limit_bytes=_vmem_limit(streamed, [], scratch)),
        name="dilated_attn",
    )(qkv, qkv, qkv, bias_tab)


def _ffn_kernel(x_ref, ya_ref, yb_ref, yc_ref, gout_ref, wout_ref, gffn_ref, wup_ref, cfw_ref, wdn_ref, fg_ref,
                *refs, layer, tiles_per_seq, final):
    g_b = gout_ref[layer:layer + 1, A_WIDTH:A_WIDTH + B_WIDTH]
    g_ffn = gffn_ref[layer:layer + 1, :]
    if final:
        o_ref, carry, act = refs
    else:
        w_next_f32, o_ref, w_next_bf16, carry, act = refs
    t_rows = x_ref.shape[0]

    @pl.when(pl.program_id(0) % tiles_per_seq == 0)
    def _():
        carry[...] = jnp.zeros((A_CARRY, 2 * D_FF), F32)

    y = jnp.concatenate([ya_ref[...], _rms(yb_ref[...], g_b).astype(BF16), yc_ref[...]], axis=-1)
    x1 = x_ref[...] + _dot(y, wout_ref[...])
    h = _rms(x1, g_ffn).astype(BF16)

    def conv3(c0):
        cols = pl.ds(c0, FF_CHUNK)
        up = _dot(h, wup_ref[:, cols])
        ext = jnp.concatenate([carry[:, cols], up], axis=0)
        carry[:, cols] = up[t_rows - A_CARRY:, :]
        w = cfw_ref[:, cols]
        return (w[0:1] * pltpu.roll(ext, 2, axis=0)[A_CARRY:, :]
                + w[1:2] * pltpu.roll(ext, 1, axis=0)[A_CARRY:, :]
                + w[2:3] * up)

    for c in range(D_FF // FF_CHUNK):
        gate = conv3(c * FF_CHUNK)
        val = conv3(D_FF + c * FF_CHUNK)
        act[:, c * FF_CHUNK:(c + 1) * FF_CHUNK] = (gate * jax.nn.sigmoid(gate) * val).astype(BF16)

    x2 = x1 + _dot(act[...], wdn_ref[...])
    if final:
        x2 = _rms(x2, fg_ref[...])
    else:
        w_next_bf16[...] = w_next_f32[...].astype(BF16)
    o_ref[...] = x2


def _ffn(x, ya, yb, yc, layer, seq, final, g_out, w_out, g_ffn, w_up, cfw, w_dn, fg, w_in):
    n = x.shape[0]
    n_steps = n // T_FFN
    tile = lambda width: pl.BlockSpec((T_FFN, width), lambda i: (i, 0))
    whole = lambda a: pl.BlockSpec(a.shape, lambda i: (0, 0))
    weight = lambda r, c: pl.BlockSpec((r, c), lambda i: (0, 0), pipeline_mode=pl.Buffered(1))
    scratch = [((A_CARRY, 2 * D_FF), F32), ((T_FFN, D_FF), BF16)]
    streamed = [((T_FFN, D_MODEL), F32)] * 2 + [((T_FFN, A_WIDTH), BF16), ((T_FFN, B_WIDTH), F32),
                                                ((T_FFN, C_WIDTH), BF16)]
    resident = [((D_MODEL, D_MODEL), BF16), ((D_MODEL, 2 * D_FF), BF16), ((D_FF, D_MODEL), BF16)]
    in_specs = [tile(D_MODEL), tile(A_WIDTH), tile(B_WIDTH), tile(C_WIDTH),
                whole(g_out), weight(D_MODEL, D_MODEL), whole(g_ffn), weight(D_MODEL, 2 * D_FF),
                pl.BlockSpec((None, FFN_CONV, 2 * D_FF), lambda i: (layer, 0, 0)),
                weight(D_FF, D_MODEL),
                whole(fg)]
    out_specs, out_shape, operands = [tile(D_MODEL)], [jax.ShapeDtypeStruct((n, D_MODEL), F32)], []
    if not final:
        src, dst, shape, blocks = _cast_slices(w_in, layer + 1, n_steps)
        in_specs, out_specs, out_shape, operands = in_specs + [src], out_specs + [dst], out_shape + [shape], [w_in]
        streamed += blocks
    return pl.pallas_call(
        functools.partial(_ffn_kernel, layer=layer, tiles_per_seq=seq // T_FFN, final=final),
        grid=(n_steps,),
        in_specs=in_specs,
        out_specs=out_specs,
        out_shape=out_shape,
        scratch_shapes=[pltpu.VMEM(*s) for s in scratch],
        compiler_params=pltpu.CompilerParams(dimension_semantics=("arbitrary",),
                                             vmem_limit_bytes=_vmem_limit(streamed, resident, scratch)),
        name="outproj_ffn",
    )(x, ya, yb, yc, g_out, w_out, g_ffn, w_up, cfw, w_dn, fg, *operands)


def kernel(x, norm_mix_g, w_in, conv_a_w, conv_c_w, conv_c_b, ln_c_g, ln_c_b, out_norm_g, w_out, norm_ffn_g, w_up, conv_f_w, w_down, rel_bias, final_g):
    batch, seq, d_model = x.shape
    depth = w_in.shape[0]
    assert d_model == D_MODEL and seq % T_IN == 0 and seq % T_FFN == 0 and seq % (BLK * max(DILATIONS)) == 0
    assert w_in.shape[2] == IN_COLS and w_up.shape[2] == 2 * D_FF and D_FF % FF_CHUNK == 0

    fg = final_g.reshape(1, D_MODEL)

    bias_tab = _bias_table(rel_bias)
    xf = x.reshape(batch * seq, D_MODEL)
    w_in_b = w_in[0].astype(BF16)
    for layer in range(depth):
        ya, qkv, yc, w_out_b, w_up_b, w_dn_b = _inproj(xf, layer, seq, norm_mix_g, w_in_b, conv_a_w, conv_c_w, conv_c_b,
                                                       ln_c_g, ln_c_b, out_norm_g, (w_out, w_up, w_down))
        yb = _attention(qkv, bias_tab).reshape(batch * seq, B_WIDTH)
        last = layer == depth - 1
        outs = _ffn(xf, ya, yb, yc, layer, seq, last, out_norm_g, w_out_b, norm_ffn_g, w_up_b, conv_f_w, w_dn_b, fg, w_in)
        xf = outs[0]
        if not last:
            w_in_b = outs[1]
    return xf.reshape(batch, seq, D_MODEL)
```

```python
import functools
import math

import numpy as np
import jax
import jax.numpy as jnp
from jax import lax
from jax.experimental import pallas as pl
from jax.experimental.pallas import tpu as pltpu

F32 = jnp.float32
BF16 = jnp.bfloat16

D_MODEL = 1024
D_HEAD = 64
A_WIDTH = 256
B_HEADS = 8
B_WIDTH = 512
C_WIDTH = 256
IN_COLS = 3 * A_WIDTH + 3 * B_WIDTH + 2 * C_WIDTH
QKV_LO = 3 * A_WIDTH
C_LO = QKV_LO + 3 * B_WIDTH
DILATIONS = (1, 4, 16)
N_KEYS = 128
BLK = 128
NUM_BUCKETS = 32
MAX_DISTANCE = 2048
SHORT_CONV = 3
CONFORMER_CONV = 31
FFN_CONV = 3
D_FF = 2816
EPS = 1e-6
NEG = -1e30
LOG2E = math.log2(math.e)

FFN_WEIGHTS = ((D_MODEL, D_MODEL), (D_MODEL, 2 * D_FF), (D_FF, D_MODEL))

LANES = 128
SUBLANES = 8
BF16_ROWS = 16
COMPILER_VMEM = 8 * 1024 * 1024

T_IN = 512
T_FFN = 512
CONV_ROWS = 64
FF_CHUNK = 256
A_CARRY = SUBLANES
C_CARRY = 32
BIAS_ROWS = 16


def _nbytes(shape, dtype):
    return math.prod(shape) * jnp.dtype(dtype).itemsize


def _cast_slices(w, layer, n_steps):
    rows, cols = w.shape[1:]
    assert rows % BF16_ROWS == 0
    pieces = math.gcd(n_steps, rows // BF16_ROWS)
    piece, repeat = rows // pieces, n_steps // pieces
    src = pl.BlockSpec((None, piece, cols), lambda i: (layer, i // repeat, 0))
    dst = pl.BlockSpec((piece, cols), lambda i: (i // repeat, 0))
    return src, dst, jax.ShapeDtypeStruct((rows, cols), BF16), [((piece, cols), F32), ((piece, cols), BF16)]


def _vmem_limit(streamed, resident, scratch):
    total = 2 * sum(_nbytes(*b) for b in streamed) + sum(_nbytes(*b) for b in resident + scratch)
    return total + COMPILER_VMEM


def _rms(x, g):
    return x * lax.rsqrt(jnp.mean(x * x, axis=-1, keepdims=True) + EPS) * g


def _dot(a, b):
    return jnp.dot(a, b, preferred_element_type=F32)


def _bucket_table():
    rel = np.arange(BLK)[:, None] - np.arange(2 * BLK)[None, :] + BLK
    valid = (rel >= 0) & (rel <= N_KEYS)
    max_exact = NUM_BUCKETS // 2
    out = []
    for d in DILATIONS:
        dist = np.maximum(rel, 0) * d
        d_f = np.maximum(dist, 1).astype(np.float32)
        large = max_exact + (np.log(d_f / np.float32(max_exact)) / np.float32(math.log(MAX_DISTANCE / max_exact))
                             * np.float32(NUM_BUCKETS - max_exact)).astype(np.int32)
        large = np.minimum(large, NUM_BUCKETS - 1)
        bucket = np.where(dist < max_exact, dist, large)
        out.append(np.where(valid, bucket, -1).astype(np.int32))
    return np.stack(out)


def _bias_kernel(rb_ref, bucket_ref, o_ref):
    for r0 in range(0, BLK, BIAS_ROWS):
        bucket = bucket_ref[r0:r0 + BIAS_ROWS, :]
        acc = [jnp.full(bucket.shape, NEG, F32) for _ in range(B_HEADS)]
        for b in range(NUM_BUCKETS):
            hit = bucket == b
            acc = [jnp.where(hit, rb_ref[b, h] * LOG2E, acc[h]) for h in range(B_HEADS)]
        for h in range(B_HEADS):
            o_ref[h, r0:r0 + BIAS_ROWS, :] = acc[h]


def _bias_table(rel_bias):
    buckets = jnp.asarray(_bucket_table())
    return pl.pallas_call(
        _bias_kernel,
        grid=(len(DILATIONS),),
        in_specs=[pl.BlockSpec(memory_space=pltpu.SMEM),
                  pl.BlockSpec((None, BLK, 2 * BLK), lambda br: (br, 0, 0))],
        out_specs=pl.BlockSpec((None, B_HEADS, BLK, 2 * BLK), lambda br: (br, 0, 0, 0)),
        out_shape=jax.ShapeDtypeStruct((len(DILATIONS), B_HEADS, BLK, 2 * BLK), F32),
        name="bias_table",
    )(rel_bias, buckets)


def _inproj_kernel(x_ref, gmix_ref, w_ref, caw_ref, ccw_ref, ccb_ref, lng_ref, lnb_ref, gout_ref, *refs,
                   layer, tiles_per_seq):
    one = lambda ref, lo=0, hi=None: ref[layer:layer + 1, lo:hi]
    g_mix, ccb, lng, lnb = one(gmix_ref), one(ccb_ref), one(lng_ref), one(lnb_ref)
    ga, gc = one(gout_ref, 0, A_WIDTH), one(gout_ref, A_WIDTH + B_WIDTH)
    n_w = len(FFN_WEIGHTS)
    w_f32, (ya_ref, qkv_ref, yc_ref), w_bf16 = refs[:n_w], refs[n_w:n_w + 3], refs[n_w + 3:2 * n_w + 3]
    tbuf, ubuf, ushift = refs[2 * n_w + 3:]
    t_rows = x_ref.shape[0]

    @pl.when(pl.program_id(0) % tiles_per_seq == 0)
    def _():
        tbuf[0:A_CARRY, :] = jnp.zeros((A_CARRY, A_WIDTH), F32)
        ubuf[0:C_CARRY, :] = jnp.zeros((C_CARRY, C_WIDTH), F32)

    h = _rms(x_ref[...], g_mix).astype(BF16)

    zc = _dot(h, w_ref[:, C_LO:IN_COLS])
    ubuf[C_CARRY:C_CARRY + t_rows, :] = zc[:, 0:C_WIDTH] * jax.nn.sigmoid(zc[:, C_WIDTH:2 * C_WIDTH])
    ccw = ccw_ref[...]
    first = C_CARRY - (CONFORMER_CONV - 1)
    shift_rows = C_CARRY + t_rows - SUBLANES
    for ph in range(1, SUBLANES):
        ushift[ph - 1, :, :] = ubuf[pl.ds(ph, shift_rows), :]
    for rc in range(t_rows // CONV_ROWS):
        acc = jnp.broadcast_to(ccb, (CONV_ROWS, C_WIDTH))
        for k in range(CONFORMER_CONV):
            tiles, ph = divmod(first + k, SUBLANES)
            rows = pl.ds(rc * CONV_ROWS + tiles * SUBLANES, CONV_ROWS)
            acc = acc + ccw[k:k + 1] * (ubuf[rows, :] if ph == 0 else ushift[ph - 1, rows, :])
        mu = jnp.mean(acc, axis=-1, keepdims=True)
        cen = acc - mu
        var = jnp.mean(cen * cen, axis=-1, keepdims=True)
        ln = cen * lax.rsqrt(var + EPS) * lng + lnb
        yc = ln * jax.nn.sigmoid(ln)
        yc_ref[rc * CONV_ROWS:(rc + 1) * CONV_ROWS, :] = _rms(yc, gc).astype(BF16)
    ubuf[0:C_CARRY, :] = ubuf[t_rows:t_rows + C_CARRY, :]

    za = _dot(h, w_ref[:, 0:QKV_LO])
    a_h, a_b, a_c = za[:, 0:A_WIDTH], za[:, A_WIDTH:2 * A_WIDTH], za[:, 2 * A_WIDTH:3 * A_WIDTH]
    t = a_c * a_h
    tbuf[A_CARRY:A_CARRY + t_rows, :] = t
    caw = caw_ref[...]
    conv = (caw[0:1] * tbuf[A_CARRY - 2:A_CARRY - 2 + t_rows, :]
            + caw[1:2] * tbuf[A_CARRY - 1:A_CARRY - 1 + t_rows, :]
            + caw[2:3] * t)
    tbuf[0:A_CARRY, :] = tbuf[t_rows:t_rows + A_CARRY, :]
    ya_ref[...] = _rms(a_b * conv, ga).astype(BF16)

    zqkv = _dot(h, w_ref[:, QKV_LO:C_LO])
    pairs = B_WIDTH // LANES
    for g in range(3 * pairs):
        slab = zqkv[:, g * LANES:(g + 1) * LANES]
        qkv_ref[g, :, :] = slab * (D_HEAD ** -0.5 * LOG2E) if g < pairs else slab

    for src, dst in zip(w_f32, w_bf16):
        dst[...] = src[...].astype(BF16)


def _inproj(x, layer, seq, g_mix, w_in, caw, ccw, ccb, lng, lnb, g_out, ffn_weights):
    n = x.shape[0]
    n_steps = n // T_IN
    tiles_per_seq = seq // T_IN
    slabs = 3 * B_WIDTH // LANES
    whole = lambda a: pl.BlockSpec(a.shape, lambda i: (0, 0))
    scratch = [((A_CARRY + T_IN, A_WIDTH), F32), ((C_CARRY + T_IN, C_WIDTH), F32),
               ((SUBLANES - 1, C_CARRY + T_IN - SUBLANES, C_WIDTH), F32)]
    streamed = [((T_IN, D_MODEL), F32), ((T_IN, A_WIDTH), BF16), ((slabs, T_IN, LANES), F32), ((T_IN, C_WIDTH), BF16)]
    w_in_specs, w_out_specs, w_shapes = [], [], []
    for w, shape in zip(ffn_weights, FFN_WEIGHTS):
        assert w.shape[1:] == shape
        src, dst, out, blocks = _cast_slices(w, layer, n_steps)
        w_in_specs.append(src)
        w_out_specs.append(dst)
        w_shapes.append(out)
        streamed += blocks
    return pl.pallas_call(
        functools.partial(_inproj_kernel, layer=layer, tiles_per_seq=tiles_per_seq),
        grid=(n_steps,),
        in_specs=[pl.BlockSpec((T_IN, D_MODEL), lambda i: (i, 0)),
                  whole(g_mix),
                  pl.BlockSpec((D_MODEL, IN_COLS), lambda i: (0, 0), pipeline_mode=pl.Buffered(1)),
                  pl.BlockSpec((None, SHORT_CONV, A_WIDTH), lambda i: (layer, 0, 0)),
                  pl.BlockSpec((None, CONFORMER_CONV, C_WIDTH), lambda i: (layer, 0, 0)),
                  whole(ccb), whole(lng), whole(lnb), whole(g_out)] + w_in_specs,
        out_specs=[pl.BlockSpec((T_IN, A_WIDTH), lambda i: (i, 0)),
                   pl.BlockSpec((None, slabs, T_IN, LANES),
                                lambda i: (i // tiles_per_seq, 0, i % tiles_per_seq, 0)),
                   pl.BlockSpec((T_IN, C_WIDTH), lambda i: (i, 0))] + w_out_specs,
        out_shape=[jax.ShapeDtypeStruct((n, A_WIDTH), BF16),
                   jax.ShapeDtypeStruct((n // seq, slabs, seq, LANES), F32),
                   jax.ShapeDtypeStruct((n, C_WIDTH), BF16)] + w_shapes,
        scratch_shapes=[pltpu.VMEM(*s) for s in scratch],
        compiler_params=pltpu.CompilerParams(
            dimension_semantics=("arbitrary",),
            vmem_limit_bytes=_vmem_limit(streamed, [((D_MODEL, IN_COLS), BF16)], scratch)),
        name="inproj_mix",
    )(x, g_mix, w_in, caw, ccw, ccb, lng, lnb, g_out, *ffn_weights)


def _attn_kernel(q_ref, k_ref, v_ref, bias_ref, o_ref, g4q, g4k, g4v, qs, ks, v0s, v1s, *stats):
    seq = q_ref.shape[0]
    n_blocks = seq // BLK
    quarter = seq // 4
    low = lax.broadcasted_iota(jnp.int32, (1, LANES), 1) < D_HEAD
    nt = (((1,), (1,)), ((), ()))
    head_lanes = (low, jnp.logical_not(low))
    v_ones = (v0s, v1s)

    def fill(dst, qv, kv, vv):
        qs[dst, :] = qv.astype(BF16)
        ks[dst, :] = kv.astype(BF16)
        v0s[dst, :] = jnp.where(low, vv, 1.0).astype(BF16)
        v1s[dst, :] = jnp.where(low, 1.0, vv).astype(BF16)

    for br, d in enumerate(DILATIONS):
        if d == 1:
            for c in range(4):
                rows = pl.ds(c * quarter, quarter)
                fill(rows, q_ref[rows, :], k_ref[rows, :], v_ref[rows, :])
        elif d == 4:
            for r in range(4):
                src, dst = pl.ds(r, quarter, stride=4), pl.ds(r * quarter, quarter)
                qv, kv, vv = q_ref[src, :], k_ref[src, :], v_ref[src, :]
                g4q[dst, :], g4k[dst, :], g4v[dst, :] = qv, kv, vv
                fill(dst, qv, kv, vv)
        else:
            for r in range(16):
                src, dst = pl.ds((r % 4) * quarter + r // 4, BLK, stride=4), pl.ds(r * BLK, BLK)
                fill(dst, g4q[src, :], g4k[src, :], g4v[src, :])

        per_class = seq // d // BLK
        blocks = []
        for b in range(n_blocks):
            r, j = divmod(b, per_class)
            rows = pl.ds(b * BLK, BLK)
            keys = rows if j == 0 else pl.ds((b - 1) * BLK, 2 * BLK)
            cols = pl.ds(BLK, BLK) if j == 0 else pl.ds(0, 2 * BLK)
            out = pl.ds((r % 4) * quarter + r // 4, BLK, stride=4) if d == 16 else rows
            blocks.append((rows, keys, cols, out))

        scores = []
        for rows, keys, cols, out in blocks:
            for hh in range(2):
                q_head = jnp.where(head_lanes[hh], qs[rows, :], jnp.zeros((), BF16))
                scores.append(lax.dot_general(q_head, ks[keys, :], nt, preferred_element_type=F32)
                              + bias_ref[br, hh, :, cols])
        probs = []
        for i, (rows, keys, cols, out) in enumerate(blocks):
            for hh in range(2):
                s = scores[2 * i + hh]
                m = jnp.max(s, axis=-1, keepdims=True)
                probs.append(jnp.exp2(s - m).astype(BF16))
                stats[2 * (2 * br + hh) + 1][out, :] = jnp.broadcast_to(m, (BLK, LANES))
        for i, (rows, keys, cols, out) in enumerate(blocks):
            for hh in range(2):
                stats[2 * (2 * br + hh)][out, :] = _dot(probs[2 * i + hh], v_ones[hh][keys, :])

    for t in range(n_blocks):
        r4, n = divmod(t, 4)
        rows4 = pl.ds(t * BLK, BLK)
        nat = pl.ds(4 * BLK * n + r4, BLK, stride=4)
        tots = []
        for hh in range(2):
            where = (nat, rows4, rows4)
            ms = [stats[2 * (2 * br + hh) + 1][where[br], :] for br in range(len(DILATIONS))]
            m_all = jnp.maximum(jnp.maximum(ms[0], ms[1]), ms[2])
            tot = jnp.zeros((BLK, LANES), F32)
            for br in range(len(DILATIONS)):
                tot = tot + jnp.exp2(ms[br] - m_all) * stats[2 * (2 * br + hh)][where[br], :]
            tots.append(tot)
        num = jnp.where(low, tots[0], tots[1])
        den = pltpu.roll(jnp.where(low, tots[1], tots[0]), D_HEAD, axis=1)
        o_ref[nat, :] = num / den


def _attention(qkv, bias_tab):
    batch, _, seq, _ = qkv.shape
    pairs = B_WIDTH // LANES
    n_br = len(DILATIONS)
    blk = lambda off: pl.BlockSpec((None, None, seq, LANES), lambda g, b: (b, off * pairs + g, 0, 0))
    scratch = [((seq, LANES), F32)] * 3 + [((seq, LANES), BF16)] * 4 + [((seq, LANES), F32)] * (4 * n_br)
    streamed = [((seq, LANES), F32)] * 4 + [((n_br, 2, BLK, 2 * BLK), F32)]
    return pl.pallas_call(
        _attn_kernel,
        grid=(pairs, batch),
        in_specs=[blk(0), blk(1), blk(2),
                  pl.BlockSpec((n_br, 2, BLK, 2 * BLK), lambda g, b: (0, g, 0, 0))],
        out_specs=pl.BlockSpec((None, seq, LANES), lambda g, b: (b, 0, g)),
        out_shape=jax.ShapeDtypeStruct((batch, seq, B_WIDTH), F32),
        scratch_shapes=[pltpu.VMEM(*s) for s in scratch],
        compiler_params=pltpu.CompilerParams(dimension_semantics=("arbitrary", "arbitrary"),
                                             vmem_limit_bytes=_vmem_limit(streamed, [], scratch)),
        name="dilated_attn",
    )(qkv, qkv, qkv, bias_tab)


def _ffn_kernel(x_ref, ya_ref, yb_ref, yc_ref, gout_ref, wout_ref, gffn_ref, wup_ref, cfw_ref, wdn_ref, fg_ref,
                *refs, layer, tiles_per_seq, final):
    g_b = gout_ref[layer:layer + 1, A_WIDTH:A_WIDTH + B_WIDTH]
    g_ffn = gffn_ref[layer:layer + 1, :]
    if final:
        o_ref, carry, act = refs
    else:
        w_next_f32, o_ref, w_next_bf16, carry, act = refs
    t_rows = x_ref.shape[0]

    @pl.when(pl.program_id(0) % tiles_per_seq == 0)
    def _():
        carry[...] = jnp.zeros((A_CARRY, 2 * D_FF), F32)

    y = jnp.concatenate([ya_ref[...], _rms(yb_ref[...], g_b).astype(BF16), yc_ref[...]], axis=-1)
    x1 = x_ref[...] + _dot(y, wout_ref[...])
    h = _rms(x1, g_ffn).astype(BF16)

    def conv3(c0):
        cols = pl.ds(c0, FF_CHUNK)
        up = _dot(h, wup_ref[:, cols])
        ext = jnp.concatenate([carry[:, cols], up], axis=0)
        carry[:, cols] = up[t_rows - A_CARRY:, :]
        w = cfw_ref[:, cols]
        return (w[0:1] * pltpu.roll(ext, 2, axis=0)[A_CARRY:, :]
                + w[1:2] * pltpu.roll(ext, 1, axis=0)[A_CARRY:, :]
                + w[2:3] * up)

    for c in range(D_FF // FF_CHUNK):
        gate = conv3(c * FF_CHUNK)
        val = conv3(D_FF + c * FF_CHUNK)
        act[:, c * FF_CHUNK:(c + 1) * FF_CHUNK] = (gate * jax.nn.sigmoid(gate) * val).astype(BF16)

    x2 = x1 + _dot(act[...], wdn_ref[...])
    if final:
        x2 = _rms(x2, fg_ref[...])
    else:
        w_next_bf16[...] = w_next_f32[...].astype(BF16)
    o_ref[...] = x2


def _ffn(x, ya, yb, yc, layer, seq, final, g_out, w_out, g_ffn, w_up, cfw, w_dn, fg, w_in):
    n = x.shape[0]
    n_steps = n // T_FFN
    tile = lambda width: pl.BlockSpec((T_FFN, width), lambda i: (i, 0))
    whole = lambda a: pl.BlockSpec(a.shape, lambda i: (0, 0))
    weight = lambda r, c: pl.BlockSpec((r, c), lambda i: (0, 0), pipeline_mode=pl.Buffered(1))
    scratch = [((A_CARRY, 2 * D_FF), F32), ((T_FFN, D_FF), BF16)]
    streamed = [((T_FFN, D_MODEL), F32)] * 2 + [((T_FFN, A_WIDTH), BF16), ((T_FFN, B_WIDTH), F32),
                                                ((T_FFN, C_WIDTH), BF16)]
    resident = [((D_MODEL, D_MODEL), BF16), ((D_MODEL, 2 * D_FF), BF16), ((D_FF, D_MODEL), BF16)]
    in_specs = [tile(D_MODEL), tile(A_WIDTH), tile(B_WIDTH), tile(C_WIDTH),
                whole(g_out), weight(D_MODEL, D_MODEL), whole(g_ffn), weight(D_MODEL, 2 * D_FF),
                pl.BlockSpec((None, FFN_CONV, 2 * D_FF), lambda i: (layer, 0, 0)),
                weight(D_FF, D_MODEL),
                whole(fg)]
    out_specs, out_shape, operands = [tile(D_MODEL)], [jax.ShapeDtypeStruct((n, D_MODEL), F32)], []
    if not final:
        src, dst, shape, blocks = _cast_slices(w_in, layer + 1, n_steps)
        in_specs, out_specs, out_shape, operands = in_specs + [src], out_specs + [dst], out_shape + [shape], [w_in]
        streamed += blocks
    return pl.pallas_call(
        functools.partial(_ffn_kernel, layer=layer, tiles_per_seq=seq // T_FFN, final=final),
        grid=(n_steps,),
        in_specs=in_specs,
        out_specs=out_specs,
        out_shape=out_shape,
        scratch_shapes=[pltpu.VMEM(*s) for s in scratch],
        compiler_params=pltpu.CompilerParams(dimension_semantics=("arbitrary",),
                                             vmem_limit_bytes=_vmem_limit(streamed, resident, scratch)),
        name="outproj_ffn",
    )(x, ya, yb, yc, g_out, w_out, g_ffn, w_up, cfw, w_dn, fg, *operands)


def kernel(x, norm_mix_g, w_in, conv_a_w, conv_c_w, conv_c_b, ln_c_g, ln_c_b, out_norm_g, w_out, norm_ffn_g, w_up, conv_f_w, w_down, rel_bias, final_g):
    batch, seq, d_model = x.shape
    depth = w_in.shape[0]
    assert d_model == D_MODEL and seq % T_IN == 0 and seq % T_FFN == 0 and seq % (BLK * max(DILATIONS)) == 0
    assert w_in.shape[2] == IN_COLS and w_up.shape[2] == 2 * D_FF and D_FF % FF_CHUNK == 0

    fg = final_g.reshape(1, D_MODEL)

    bias_tab = _bias_table(rel_bias)
    xf = x.reshape(batch * seq, D_MODEL)
    w_in_b = w_in[0].astype(BF16)
    for layer in range(depth):
        ya, qkv, yc, w_out_b, w_up_b, w_dn_b = _inproj(xf, layer, seq, norm_mix_g, w_in_b, conv_a_w, conv_c_w, conv_c_b,
                                                       ln_c_g, ln_c_b, out_norm_g, (w_out, w_up, w_down))
        yb = _attention(qkv, bias_tab).reshape(batch * seq, B_WIDTH)
        last = layer == depth - 1
        outs = _ffn(xf, ya, yb, yc, layer, seq, last, out_norm_g, w_out_b, norm_ffn_g, w_up_b, conv_f_w, w_dn_b, fg, w_in)
        xf = outs[0]
        if not last:
            w_in_b = outs[1]
    return xf.reshape(batch, seq, D_MODEL)
```
